```python
import math
import jax, jax.numpy as jnp
from jax import lax
import numpy as np

D_MODEL = 1024
BATCH = 32
SEQ = 2048
DEPTH = 2

N_MEM = 256
HEAD_DIM = 64
ROPE_DIM = HEAD_DIM // 4
ROPE_THETA = 500000.0
RMS_EPS = 1e-6
N_TOK_HEADS = 12
N_MEM_HEADS = 4
MIX_WIDTH = (N_TOK_HEADS + N_MEM_HEADS) * HEAD_DIM
MEM_Q = N_MEM_HEADS * HEAD_DIM
D_FF = ((8 * D_MODEL // 3 + 255) // 256) * 256
Q_BLOCK = 128

DSA_KV_RANK = 128
DSA_NOPE = HEAD_DIM - ROPE_DIM
IDX_HEADS = 8
IDX_DIM = 64
DSA_TOPK_MAX = 256
DSA_COLS = [N_TOK_HEADS * HEAD_DIM, DSA_KV_RANK, ROPE_DIM,
            IDX_HEADS * IDX_DIM, IDX_DIM, IDX_HEADS, MEM_Q]

NSA_GROUPS = 2
NSA_HPG = N_TOK_HEADS // NSA_GROUPS
CMP_LEN = 32
CMP_STRIDE = 16
CMP_HIDDEN = 128
SLC_LEN = 64
SLC_TOP_MAX = 16
WIN = 512
NSA_Q_BLOCK = 64
FORCE_SCORE = 1e9
NSA_KV = NSA_GROUPS * HEAD_DIM
NSA_COLS = [N_TOK_HEADS * HEAD_DIM, NSA_KV, NSA_KV, NSA_KV, NSA_KV, NSA_KV, NSA_KV,
            N_TOK_HEADS * 3, MEM_Q]

N_A = (DEPTH + 1) // 2
N_B = DEPTH // 2

kernel_name = "hybrid_dsa_nsa_memory_trunk"


def split_cols(x, sizes):
    return jnp.split(x, np.cumsum(sizes)[:-1].tolist(), axis=-1)


def rmsnorm(x, g):
    xf = x.astype(jnp.float32)
    y = xf * lax.rsqrt(jnp.mean(xf * xf, axis=-1, keepdims=True) + RMS_EPS)
    return (y * g.astype(jnp.float32)).astype(x.dtype)


def rope_tables(S):
    inv = ROPE_THETA ** (-np.arange(0, ROPE_DIM, 2, dtype=np.float64) / ROPE_DIM)
    ang = np.arange(S, dtype=np.float64)[:, None] * inv[None, :]
    return jnp.asarray(np.cos(ang).astype(np.float32)), jnp.asarray(np.sin(ang).astype(np.float32))


def apply_partial_rope(x, cos, sin):
    xr, xp = x[..., :ROPE_DIM], x[..., ROPE_DIM:]
    x1, x2 = xr[..., :ROPE_DIM // 2], xr[..., ROPE_DIM // 2:]
    c, s = cos[:, None, :], sin[:, None, :]
    rot = jnp.concatenate([x1 * c - x2 * s, x1 * s + x2 * c], axis=-1).astype(x.dtype)
    return jnp.concatenate([rot, xp], axis=-1)


def masked_softmax(s, mask):
    s = jnp.where(mask, s.astype(jnp.float32), -jnp.inf)
    m = jnp.max(s, axis=-1, keepdims=True)
    m = jnp.where(jnp.isfinite(m), m, 0.0)
    p = jnp.exp(s - m)
    return p / jnp.maximum(jnp.sum(p, axis=-1, keepdims=True), 1e-30)


def memory_attention(q_mem, mem_n, w_kv_mem):
    B, S, _ = q_mem.shape
    q = q_mem.reshape(B, S, N_MEM_HEADS, HEAD_DIM)
    k, v = jnp.split(mem_n @ w_kv_mem, 2, axis=-1)
    k = k.reshape(B, -1, N_MEM_HEADS, HEAD_DIM)
    v = v.reshape(B, -1, N_MEM_HEADS, HEAD_DIM)
    s = jnp.einsum('bshd,bnhd->bhsn', q, k).astype(jnp.float32) * HEAD_DIM ** -0.5
    p = jax.nn.softmax(s, axis=-1).astype(v.dtype)
    return jnp.einsum('bhsn,bnhd->bshd', p, v).reshape(B, S, MEM_Q)


def dsa_mixer(cols, ckv_norm, w_uk, w_uv, cos, sin):
    q, ckv, k_rope, q_idx, k_idx, w_idx = cols
    B, S, _ = q.shape
    q = apply_partial_rope(q.reshape(B, S, N_TOK_HEADS, HEAD_DIM), cos, sin)
    ckv = rmsnorm(ckv, ckv_norm)
    k_nope = ckv @ w_uk
    v = ckv @ w_uv
    k_r = apply_partial_rope(k_rope[:, :, None, :], cos, sin)[:, :, 0]
    k = jnp.concatenate([k_r, k_nope], axis=-1)
    q_idx = apply_partial_rope(q_idx.reshape(B, S, IDX_HEADS, IDX_DIM), cos, sin)
    k_idx = apply_partial_rope(k_idx[:, :, None, :], cos, sin)[:, :, 0]
    w_idx = w_idx.astype(jnp.float32) * (IDX_HEADS ** -0.5 * IDX_DIM ** -0.5)
    topk = min(DSA_TOPK_MAX, S // 4)
    key_pos = jnp.arange(S)
    gather = jax.vmap(lambda kb, ib: kb[ib])

    def block(i):
        t0 = i * Q_BLOCK
        qpos = t0 + jnp.arange(Q_BLOCK)
        qi = lax.dynamic_slice_in_dim(q, t0, Q_BLOCK, axis=1)
        qii = lax.dynamic_slice_in_dim(q_idx, t0, Q_BLOCK, axis=1)
        wi = lax.dynamic_slice_in_dim(w_idx, t0, Q_BLOCK, axis=1)
        logits = jnp.einsum('bthd,bsd->bths', qii, k_idx).astype(jnp.float32)
        score = jnp.einsum('bths,bth->bts', jax.nn.relu(logits), wi)
        causal = key_pos[None, :] <= qpos[:, None]
        score = jnp.where(causal[None], score, -jnp.inf)
        _, idx = lax.top_k(score, topk)
        k_sel = gather(k, idx)
        v_sel = gather(v, idx)
        valid = idx <= qpos[None, :, None]
        s = jnp.einsum('bthd,btkd->bthk', qi, k_sel) * HEAD_DIM ** -0.5
        p = masked_softmax(s, valid[:, :, None, :]).astype(v.dtype)
        return jnp.einsum('bthk,btkd->bthd', p, v_sel).reshape(B, Q_BLOCK, N_TOK_HEADS * HEAD_DIM)

    out = lax.map(block, jnp.arange(S // Q_BLOCK))
    return out.transpose(1, 0, 2, 3).reshape(B, S, N_TOK_HEADS * HEAD_DIM)


def cmp_to_slc_overlap(n_c, n_slc):
    c0 = np.arange(n_c) * CMP_STRIDE
    s0 = np.arange(n_slc) * SLC_LEN
    ov = np.minimum(c0[:, None] + CMP_LEN, s0[None, :] + SLC_LEN) - np.maximum(c0[:, None], s0[None, :])
    return jnp.asarray((np.clip(ov, 0, None) / CMP_LEN).astype(np.float32))


def compress(kv, pos_emb, w1, w2):
    B, S, G, Dh = kv.shape
    n_c = (S - CMP_LEN) // CMP_STRIDE + 1
    blk = np.arange(n_c)[:, None] * CMP_STRIDE + np.arange(CMP_LEN)[None, :]
    blocks = kv[:, blk] + pos_emb[None, None, :, None, :]
    flat = blocks.transpose(0, 1, 3, 2, 4).reshape(B, n_c, G, CMP_LEN * Dh)
    return jax.nn.gelu(flat @ w1) @ w2


def nsa_mixer(cols, pos_k, pos_v, ck_w1, ck_w2, cv_w1, cv_w2, cos, sin):
    q, kc, vc, ks, vs, kw, vw, gates = cols
    B, S, _ = q.shape
    G, J, Dh = NSA_GROUPS, NSA_HPG, HEAD_DIM
    q = q.reshape(B, S, N_TOK_HEADS, Dh)
    q_rot = apply_partial_rope(q, cos, sin)
    kc, vc = kc.reshape(B, S, G, Dh), vc.reshape(B, S, G, Dh)
    ks = apply_partial_rope(ks.reshape(B, S, G, Dh), cos, sin)
    vs = vs.reshape(B, S, G, Dh)
    kw = apply_partial_rope(kw.reshape(B, S, G, Dh), cos, sin)
    vw = vw.reshape(B, S, G, Dh)
    gates = jax.nn.sigmoid(gates.astype(jnp.float32)).astype(q.dtype).reshape(B, S, G, J, 3)
    scale = Dh ** -0.5

    k_cmp = compress(kc, pos_k, ck_w1, ck_w2)
    v_cmp = compress(vc, pos_v, cv_w1, cv_w2)
    n_c = k_cmp.shape[1]
    cmp_end = jnp.arange(n_c) * CMP_STRIDE + CMP_LEN - 1
    n_slc = S // SLC_LEN
    n_sel = min(SLC_TOP_MAX, n_slc)
    overlap = cmp_to_slc_overlap(n_c, n_slc)
    slc_ids = jnp.arange(n_slc)
    ks_b = ks.reshape(B, n_slc, SLC_LEN, G, Dh).transpose(0, 3, 1, 2, 4)
    vs_b = vs.reshape(B, n_slc, SLC_LEN, G, Dh).transpose(0, 3, 1, 2, 4)
    gather = jax.vmap(jax.vmap(lambda kb, ib: kb[ib]))
    kw_pad = jnp.pad(kw, ((0, 0), (WIN, 0), (0, 0), (0, 0)))
    vw_pad = jnp.pad(vw, ((0, 0), (WIN, 0), (0, 0), (0, 0)))

    def block(i):
        T = NSA_Q_BLOCK
        t0 = i * T
        qpos = t0 + jnp.arange(T)
        qc = lax.dynamic_slice_in_dim(q, t0, T, axis=1).reshape(B, T, G, J, Dh)
        qr = lax.dynamic_slice_in_dim(q_rot, t0, T, axis=1).reshape(B, T, G, J, Dh)
        gi = lax.dynamic_slice_in_dim(gates, t0, T, axis=1)
        s_c = jnp.einsum('btgjd,bcgd->btgjc', qc, k_cmp) * scale
        mask_c = cmp_end[None, :] <= qpos[:, None]
        p_c = masked_softmax(s_c, mask_c[None, :, None, None, :])
        o_c = jnp.einsum('btgjc,bcgd->btgjd', p_c.astype(v_cmp.dtype), v_cmp)
        imp = jnp.einsum('btgjc,cn->btgn', p_c, overlap)
        cur = qpos // SLC_LEN
        forced = (slc_ids[None] == 0) | (slc_ids[None] == cur[:, None]) | (slc_ids[None] == cur[:, None] - 1)
        admissible = slc_ids[None] <= cur[:, None]
        imp = jnp.where(forced[None, :, None, :], FORCE_SCORE, imp)
        imp = jnp.where(admissible[None, :, None, :], imp, -jnp.inf)
        _, sel = lax.top_k(imp, n_sel)
        sel_g = sel.transpose(0, 2, 1, 3)
        k_sel = gather(ks_b, sel_g).reshape(B, G, T, n_sel * SLC_LEN, Dh)
        v_sel = gather(vs_b, sel_g).reshape(B, G, T, n_sel * SLC_LEN, Dh)
        tok_pos = (sel_g[..., None] * SLC_LEN + jnp.arange(SLC_LEN)).reshape(B, G, T, n_sel * SLC_LEN)
        mask_s = tok_pos <= qpos[None, None, :, None]
        s_s = jnp.einsum('btgjd,bgtkd->bgtjk', qr, k_sel) * scale
        p_s = masked_softmax(s_s, mask_s[:, :, :, None, :]).astype(v_sel.dtype)
        o_s = jnp.einsum('bgtjk,bgtkd->btgjd', p_s, v_sel)
        kw_i = lax.dynamic_slice_in_dim(kw_pad, t0, WIN + T, axis=1)
        vw_i = lax.dynamic_slice_in_dim(vw_pad, t0, WIN + T, axis=1)
        kpos = t0 - WIN + jnp.arange(WIN + T)
        mask_w = (kpos[None] <= qpos[:, None]) & (kpos[None] > qpos[:, None] - WIN) & (kpos[None] >= 0)
        s_w = jnp.einsum('btgjd,bkgd->btgjk', qr, kw_i) * scale
        p_w = masked_softmax(s_w, mask_w[None, :, None, None, :]).astype(vw_i.dtype)
        o_w = jnp.einsum('btgjk,bkgd->btgjd', p_w, vw_i)
        o = gi[..., 0:1] * o_c + gi[..., 1:2] * o_s + gi[..., 2:3] * o_w
        return o.reshape(B, T, N_TOK_HEADS * Dh)

    out = lax.map(block, jnp.arange(S // NSA_Q_BLOCK))
    return out.transpose(1, 0, 2, 3).reshape(B, S, N_TOK_HEADS * HEAD_DIM)


def setup_inputs(seed: int = 0) -> dict:
    key = jax.random.key(seed)
    ks = jax.random.split(key, 24)

    def w(k, shape, fan_in):
        return jax.random.normal(k, shape, jnp.float32) * fan_in ** -0.5

    def gain(k, shape):
        return 1.0 + 0.02 * jax.random.normal(k, shape, jnp.float32)

    return {
        "x": jax.random.normal(ks[0], (BATCH, SEQ, D_MODEL), jnp.float32),
        "mem": jax.random.normal(ks[1], (BATCH, N_MEM, D_MODEL), jnp.float32),
        "attn_norm": gain(ks[2], (DEPTH, D_MODEL)),
        "mem_norm": gain(ks[3], (DEPTH, D_MODEL)),
        "ffn_norm": gain(ks[4], (DEPTH, D_MODEL)),
        "final_norm": gain(ks[5], (D_MODEL,)),
        "dsa_w_in": w(ks[6], (N_A, D_MODEL, sum(DSA_COLS)), D_MODEL),
        "dsa_ckv_norm": gain(ks[7], (N_A, DSA_KV_RANK)),
        "dsa_w_uk": w(ks[8], (N_A, DSA_KV_RANK, DSA_NOPE), DSA_KV_RANK),
        "dsa_w_uv": w(ks[9], (N_A, DSA_KV_RANK, HEAD_DIM), DSA_KV_RANK),
        "nsa_w_in": w(ks[10], (N_B, D_MODEL, sum(NSA_COLS)), D_MODEL),
        "nsa_cmp_pos_k": 0.1 * jax.random.normal(ks[11], (N_B, CMP_LEN, HEAD_DIM), jnp.float32),
        "nsa_cmp_pos_v": 0.1 * jax.random.normal(ks[12], (N_B, CMP_LEN, HEAD_DIM), jnp.float32),
        "nsa_cmp_k_w1": w(ks[13], (N_B, CMP_LEN * HEAD_DIM, CMP_HIDDEN), CMP_LEN * HEAD_DIM),
        "nsa_cmp_k_w2": w(ks[14], (N_B, CMP_HIDDEN, HEAD_DIM), CMP_HIDDEN),
        "nsa_cmp_v_w1": w(ks[15], (N_B, CMP_LEN * HEAD_DIM, CMP_HIDDEN), CMP_LEN * HEAD_DIM),
        "nsa_cmp_v_w2": w(ks[16], (N_B, CMP_HIDDEN, HEAD_DIM), CMP_HIDDEN),
        "mem_w_kv": w(ks[17], (DEPTH, D_MODEL, 2 * MEM_Q), D_MODEL),
        "w_o": w(ks[18], (DEPTH, MIX_WIDTH, D_MODEL), MIX_WIDTH),
        "ffn_w_in": w(ks[19], (DEPTH, D_MODEL, 2 * D_FF), D_MODEL),
        "ffn_w_down": w(ks[20], (DEPTH, D_FF, D_MODEL), D_FF),
    }


def reference(x, mem, attn_norm, mem_norm, ffn_norm, final_norm,
              dsa_w_in, dsa_ckv_norm, dsa_w_uk, dsa_w_uv,
              nsa_w_in, nsa_cmp_pos_k, nsa_cmp_pos_v,
              nsa_cmp_k_w1, nsa_cmp_k_w2, nsa_cmp_v_w1, nsa_cmp_v_w2,
              mem_w_kv, w_o, ffn_w_in, ffn_w_down):
    S = x.shape[1]
    cos, sin = rope_tables(S)
    for i in range(DEPTH):
        h = rmsnorm(x, attn_norm[i])
        mem_n = rmsnorm(mem, mem_norm[i])
        if i % 2 == 0:
            a = i // 2
            cols = split_cols(h @ dsa_w_in[a], DSA_COLS)
            q_mem = cols[-1]
            tok = dsa_mixer(cols[:-1], dsa_ckv_norm[a], dsa_w_uk[a], dsa_w_uv[a], cos, sin)
        else:
            b = i // 2
            cols = split_cols(h @ nsa_w_in[b], NSA_COLS)
            q_mem = cols[-1]
            tok = nsa_mixer(cols[:-1], nsa_cmp_pos_k[b], nsa_cmp_pos_v[b],
                            nsa_cmp_k_w1[b], nsa_cmp_k_w2[b], nsa_cmp_v_w1[b], nsa_cmp_v_w2[b],
                            cos, sin)
        o_mem = memory_attention(q_mem, mem_n, mem_w_kv[i])
        x = x + jnp.concatenate([tok, o_mem], axis=-1) @ w_o[i]
        h = rmsnorm(x, ffn_norm[i])
        g, u = jnp.split(h @ ffn_w_in[i], 2, axis=-1)
        x = x + (jax.nn.silu(g) * u) @ ffn_w_down[i]
    return rmsnorm(x, final_norm)
```

```python
import functools
import math

import numpy as np
import jax
import jax.numpy as jnp
from jax import lax
from jax.experimental import pallas as pl
from jax.experimental.pallas import tpu as pltpu

F32 = jnp.float32
BF16 = jnp.bfloat16

HEAD_DIM = 64
ROPE_DIM = 16
ROPE_THETA = 500000.0
RMS_EPS = 1e-6
N_TOK_HEADS = 12
N_MEM_HEADS = 4
TOK_W = N_TOK_HEADS * HEAD_DIM
MEM_Q = N_MEM_HEADS * HEAD_DIM
DSA_KV_RANK = 128
DSA_NOPE = HEAD_DIM - ROPE_DIM
IDX_HEADS = 8
IDX_DIM = 64
DSA_TOPK_MAX = 256
NSA_GROUPS = 2
NSA_HPG = N_TOK_HEADS // NSA_GROUPS
CMP_LEN = 32
CMP_STRIDE = 16
CMP_HIDDEN = 128
SLC_LEN = 64
SLC_TOP_MAX = 16
WIN = 512
FORCE_SCORE = 1e9
N_GATES = N_TOK_HEADS * 3
GATE_ROWS = 40

LANE = 128
TQ = 128
TK = 128
TM_PROJ = 512
TM_FFN = 256
VMEM_LIMIT = 56 * 1024 * 1024
NEG = -1e30
INT_MIN = -2 ** 31

_NT = (((1,), (1,)), ((), ()))
_TN = (((0,), (0,)), ((), ()))


def _cparams(n_axes):
    return pltpu.CompilerParams(
        dimension_semantics=("arbitrary",) * n_axes, vmem_limit_bytes=VMEM_LIMIT)


def _rms(xf, g):
    ms = jnp.mean(xf * xf, axis=-1, keepdims=True)
    return xf * lax.rsqrt(ms + RMS_EPS) * g


def _rope_tables(S):
    inv = ROPE_THETA ** (-np.arange(0, ROPE_DIM, 2, dtype=np.float64) / ROPE_DIM)
    ang = np.arange(S, dtype=np.float64)[:, None] * inv[None, :]
    cos = np.cos(ang).astype(np.float32)
    sin = np.sin(ang).astype(np.float32)
    half = ROPE_DIM // 2
    c = np.ones((S, LANE), np.float32)
    s_up = np.zeros((S, LANE), np.float32)
    s_dn = np.zeros((S, LANE), np.float32)
    for base in range(0, LANE, HEAD_DIM):
        c[:, base:base + half] = cos
        c[:, base + half:base + ROPE_DIM] = cos
        s_up[:, base + half:base + ROPE_DIM] = sin
        s_dn[:, base:base + half] = -sin
    return jnp.asarray(c), jnp.asarray(s_up), jnp.asarray(s_dn)


def _rope(x, c, s_up, s_dn):
    half = ROPE_DIM // 2
    outs = []
    for j in range(x.shape[1] // LANE):
        xc = x[:, j * LANE:(j + 1) * LANE]
        outs.append(xc * c + pltpu.roll(xc, half, 1) * s_up + pltpu.roll(xc, LANE - half, 1) * s_dn)
    return outs


def _store_heads(ref, chunks, dtype):
    for j, ch in enumerate(chunks):
        ref[0, 2 * j, :, :] = ch[:, :HEAD_DIM].astype(dtype)
        ref[0, 2 * j + 1, :, :] = ch[:, HEAD_DIM:].astype(dtype)


def _chunks(x):
    return [x[:, j * LANE:(j + 1) * LANE] for j in range(x.shape[1] // LANE)]


def _proj_dsa_kernel(x_ref, g_ref, w_ref, gkv_ref, wkv_ref, c_ref, su_ref, sd_ref,
                     q_ref, qi_ref, kidx_ref, k_ref, v_ref, wt_ref, qm_ref):
    h = _rms(x_ref[...], g_ref[...]).astype(BF16)
    c, su, sd = c_ref[...], su_ref[...], sd_ref[...]

    def mm(lo, hi):
        return jnp.dot(h, w_ref[:, lo:hi], preferred_element_type=F32)

    scale = HEAD_DIM ** -0.5
    q = _rope(mm(0, TOK_W), c, su, sd)
    _store_heads(q_ref, [ch * scale for ch in q], BF16)

    ckv = _rms(mm(TOK_W, TOK_W + DSA_KV_RANK), gkv_ref[...]).astype(BF16)
    kvn = jnp.dot(ckv, wkv_ref[...], preferred_element_type=F32)

    o_qi = TOK_W + DSA_KV_RANK
    qi = _rope(mm(o_qi, o_qi + IDX_HEADS * IDX_DIM), c, su, sd)
    _store_heads(qi_ref, qi, BF16)

    o_misc = o_qi + IDX_HEADS * IDX_DIM
    misc = _rope(mm(o_misc, o_misc + LANE), c, su, sd)[0]
    kidx_ref[0] = misc[:, :IDX_DIM].astype(BF16)
    lane = lax.broadcasted_iota(jnp.int32, misc.shape, 1)
    k_full = jnp.where(lane < ROPE_DIM, pltpu.roll(misc, HEAD_DIM, 1), kvn)
    k_ref[0] = k_full[:, :HEAD_DIM].astype(BF16)
    v_ref[0] = kvn[:, HEAD_DIM:].astype(BF16)
    w_scaled = misc * (IDX_HEADS ** -0.5 * IDX_DIM ** -0.5)
    w_row0 = IDX_DIM + ROPE_DIM
    wt_ref[0] = w_scaled.T[w_row0:w_row0 + IDX_HEADS, :]

    o_qm = o_misc + LANE
    qm = mm(o_qm, o_qm + MEM_Q)
    _store_heads(qm_ref, [ch * scale for ch in _chunks(qm)], BF16)


def _proj_dsa(x2d, g, w_in, gkv, w_uk, w_uv, tabs, B, S):
    D = x2d.shape[1]
    o = np.cumsum([0, TOK_W, DSA_KV_RANK, ROPE_DIM, IDX_HEADS * IDX_DIM, IDX_DIM, IDX_HEADS, MEM_Q])
    q, ckv, kr, qi, ki, wi, qm = [w_in[:, o[i]:o[i + 1]] for i in range(7)]
    pad = jnp.zeros((D, LANE - IDX_DIM - ROPE_DIM - IDX_HEADS), w_in.dtype)
    w = jnp.concatenate([q, ckv, qi, ki, kr, wi, pad, qm], axis=1).astype(BF16)
    wkv = jnp.concatenate([jnp.zeros((DSA_KV_RANK, ROPE_DIM), F32), w_uk, w_uv], axis=1).astype(BF16)
    ncol = w.shape[1]
    tm = min(TM_PROJ, S)
    nst = S // tm
    tok = lambda i: (i, 0)
    const = lambda i: (0, 0)
    tab = lambda i: (i % nst, 0)
    hm = lambda i: (i // nst, 0, i % nst, 0)
    row = lambda i: (i // nst, i % nst, 0)
    outs = pl.pallas_call(
        _proj_dsa_kernel,
        grid=(B * nst,),
        in_specs=[
            pl.BlockSpec((tm, D), tok),
            pl.BlockSpec((1, D), const),
            pl.BlockSpec((D, ncol), const),
            pl.BlockSpec((1, DSA_KV_RANK), const),
            pl.BlockSpec((DSA_KV_RANK, LANE), const),
            pl.BlockSpec((tm, LANE), tab),
            pl.BlockSpec((tm, LANE), tab),
            pl.BlockSpec((tm, LANE), tab),
        ],
        out_specs=[
            pl.BlockSpec((1, N_TOK_HEADS, tm, HEAD_DIM), hm),
            pl.BlockSpec((1, IDX_HEADS, tm, IDX_DIM), hm),
            pl.BlockSpec((1, tm, IDX_DIM), row),
            pl.BlockSpec((1, tm, HEAD_DIM), row),
            pl.BlockSpec((1, tm, HEAD_DIM), row),
            pl.BlockSpec((1, IDX_HEADS, tm), lambda i: (i // nst, 0, i % nst)),
            pl.BlockSpec((1, N_MEM_HEADS, tm, HEAD_DIM), hm),
        ],
        out_shape=[
            jax.ShapeDtypeStruct((B, N_TOK_HEADS, S, HEAD_DIM), BF16),
            jax.ShapeDtypeStruct((B, IDX_HEADS, S, IDX_DIM), BF16),
            jax.ShapeDtypeStruct((B, S, IDX_DIM), BF16),
            jax.ShapeDtypeStruct((B, S, HEAD_DIM), BF16),
            jax.ShapeDtypeStruct((B, S, HEAD_DIM), BF16),
            jax.ShapeDtypeStruct((B, IDX_HEADS, S), F32),
            jax.ShapeDtypeStruct((B, N_MEM_HEADS, S, HEAD_DIM), BF16),
        ],
        compiler_params=_cparams(1),
        name="proj_dsa",
    )(x2d, g.reshape(1, D), w, gkv.reshape(1, DSA_KV_RANK), wkv, *tabs)
    return outs


def _proj_nsa_kernel(x_ref, g_ref, w_ref, c_ref, su_ref, sd_ref,
                     qraw_ref, qrot_ref, kc_ref, vc_ref, ks_ref, vs_ref, kw_ref, vw_ref,
                     gt_ref, qm_ref):
    h = _rms(x_ref[...], g_ref[...]).astype(BF16)
    c, su, sd = c_ref[...], su_ref[...], sd_ref[...]

    def mm(lo, hi):
        return jnp.dot(h, w_ref[:, lo:hi], preferred_element_type=F32)

    scale = HEAD_DIM ** -0.5
    q = mm(0, TOK_W)
    _store_heads(qraw_ref, [ch * scale for ch in _chunks(q)], BF16)
    _store_heads(qrot_ref, [ch * scale for ch in _rope(q, c, su, sd)], BF16)

    o = TOK_W
    _store_heads(kc_ref, _chunks(mm(o, o + LANE)), F32)
    _store_heads(vc_ref, _chunks(mm(o + LANE, o + 2 * LANE)), F32)
    _store_heads(ks_ref, _rope(mm(o + 2 * LANE, o + 3 * LANE), c, su, sd), BF16)
    _store_heads(vs_ref, _chunks(mm(o + 3 * LANE, o + 4 * LANE)), BF16)
    _store_heads(kw_ref, _rope(mm(o + 4 * LANE, o + 5 * LANE), c, su, sd), BF16)
    _store_heads(vw_ref, _chunks(mm(o + 5 * LANE, o + 6 * LANE)), BF16)

    gates = jax.nn.sigmoid(mm(o + 6 * LANE, o + 7 * LANE))
    gt_ref[0] = gates.T[:GATE_ROWS, :]

    qm = mm(o + 7 * LANE, o + 7 * LANE + MEM_Q)
    _store_heads(qm_ref, [ch * scale for ch in _chunks(qm)], BF16)


def _proj_nsa(x2d, g, w_in, tabs, B, S):
    D = x2d.shape[1]
    kv_w = NSA_GROUPS * HEAD_DIM
    o_g = TOK_W + 6 * kv_w
    pad = jnp.zeros((D, LANE - N_GATES), w_in.dtype)
    w = jnp.concatenate([w_in[:, :o_g + N_GATES], pad, w_in[:, o_g + N_GATES:]], axis=1).astype(BF16)
    ncol = w.shape[1]
    tm = min(TM_PROJ, S)
    nst = S // tm
    tok = lambda i: (i, 0)
    const = lambda i: (0, 0)
    tab = lambda i: (i % nst, 0)
    hm = lambda i: (i // nst, 0, i % nst, 0)

    def hm_spec(nh):
        return pl.BlockSpec((1, nh, tm, HEAD_DIM), hm)

    def hm_shape(nh, dt):
        return jax.ShapeDtypeStruct((B, nh, S, HEAD_DIM), dt)

    G = NSA_GROUPS
    outs = pl.pallas_call(
        _proj_nsa_kernel,
        grid=(B * nst,),
        in_specs=[
            pl.BlockSpec((tm, D), tok),
            pl.BlockSpec((1, D), const),
            pl.BlockSpec((D, ncol), const),
            pl.BlockSpec((tm, LANE), tab),
            pl.BlockSpec((tm, LANE), tab),
            pl.BlockSpec((tm, LANE), tab),
        ],
        out_specs=[
            hm_spec(N_TOK_HEADS), hm_spec(N_TOK_HEADS),
            hm_spec(G), hm_spec(G), hm_spec(G), hm_spec(G), hm_spec(G), hm_spec(G),
            pl.BlockSpec((1, GATE_ROWS, tm), lambda i: (i // nst, 0, i % nst)),
            hm_spec(N_MEM_HEADS),
        ],
        out_shape=[
            hm_shape(N_TOK_HEADS, BF16), hm_shape(N_TOK_HEADS, BF16),
            hm_shape(G, F32), hm_shape(G, F32),
            hm_shape(G, BF16), hm_shape(G, BF16), hm_shape(G, BF16), hm_shape(G, BF16),
            jax.ShapeDtypeStruct((B, GATE_ROWS, S), F32),
            hm_shape(N_MEM_HEADS, BF16),
        ],
        compiler_params=_cparams(1),
        name="proj_nsa",
    )(x2d, g.reshape(1, D), w, *tabs)
    return outs


def _mem_kv_kernel(m_ref, g_ref, w_ref, k_ref, v_ref):
    h = _rms(m_ref[0], g_ref[...]).astype(BF16)
    kv = jnp.dot(h, w_ref[...], preferred_element_type=F32)
    _store_heads(k_ref, _chunks(kv[:, :MEM_Q]), BF16)
    _store_heads(v_ref, _chunks(kv[:, MEM_Q:]), BF16)


def _mem_kv(mem, g, w_kv):
    B, NM, D = mem.shape
    spec = pl.BlockSpec((1, N_MEM_HEADS, NM, HEAD_DIM), lambda b: (b, 0, 0, 0))
    shape = jax.ShapeDtypeStruct((B, N_MEM_HEADS, NM, HEAD_DIM), BF16)
    return pl.pallas_call(
        _mem_kv_kernel,
        grid=(B,),
        in_specs=[
            pl.BlockSpec((1, NM, D), lambda b: (b, 0, 0)),
            pl.BlockSpec((1, D), lambda b: (0, 0)),
            pl.BlockSpec((D, 2 * MEM_Q), lambda b: (0, 0)),
        ],
        out_specs=[spec, spec],
        out_shape=[shape, shape],
        compiler_params=_cparams(1),
        name="mem_kv",
    )(mem, g.reshape(1, D), w_kv.astype(BF16))


def _write_heads_t(out_ref, col0, acc_t, n_heads):
    for h in range(0, n_heads, 2):
        pair = jnp.concatenate(
            [acc_t[:, h * TQ:(h + 1) * TQ], acc_t[:, (h + 1) * TQ:(h + 2) * TQ]], axis=0)
        out_ref[0, :, col0 + h * HEAD_DIM:col0 + (h + 2) * HEAD_DIM] = pair.T.astype(out_ref.dtype)


def _mem_attention_t(qm_ref, km_ref, vm_ref):
    outs = []
    for h in range(N_MEM_HEADS):
        s = lax.dot_general(km_ref[0, h], qm_ref[0, h], _NT, preferred_element_type=F32)
        m = jnp.max(s, axis=0, keepdims=True)
        p = jnp.exp(s - m)
        l = jnp.sum(p, axis=0, keepdims=True)
        o = lax.dot_general(vm_ref[0, h], p.astype(BF16), _TN, preferred_element_type=F32)
        outs.append(o / l)
    return jnp.concatenate(outs, axis=1)


def _online_step(s, v_chunk, carry):
    m, l, acc = carry
    m_new = jnp.maximum(m, jnp.max(s, axis=0, keepdims=True))
    alpha = jnp.exp(m - m_new)
    p = jnp.exp(s - m_new)
    l = alpha * l + jnp.sum(p, axis=0, keepdims=True)
    pv = lax.dot_general(v_chunk, p.astype(BF16), _TN, preferred_element_type=F32)
    return m_new, l, alpha * acc + pv


def _online_init(L):
    return (jnp.full((1, L), NEG, F32), jnp.zeros((1, L), F32), jnp.zeros((HEAD_DIM, L), F32))


def _dsa_attn_kernel(q_ref, qi_ref, wt_ref, kidx_ref, k_ref, v_ref, qm_ref, km_ref, vm_ref,
                     out_ref, key_s, *, topk):
    i = pl.program_id(1)
    nk = i + 1
    t_pos = i * TQ + lax.broadcasted_iota(jnp.int32, (TK, TQ), 1)
    row = lax.broadcasted_iota(jnp.int32, (TK, TQ), 0)
    wt = wt_ref[0]
    qi_all = qi_ref[0].reshape(IDX_HEADS * TQ, IDX_DIM)
    q_all = q_ref[0].reshape(N_TOK_HEADS * TQ, HEAD_DIM)

    def score_body(kc, _):
        k0 = pl.multiple_of(kc * TK, TK)
        lg = lax.dot_general(kidx_ref[0, pl.ds(k0, TK), :], qi_all, _NT,
                             preferred_element_type=F32)
        sc = jnp.zeros((TK, TQ), F32)
        for h in range(IDX_HEADS):
            sc = sc + jnp.maximum(lg[:, h * TQ:(h + 1) * TQ], 0.0) * wt[h:h + 1, :]
        sc = jnp.where(sc == 0.0, 0.0, sc)
        bits = pltpu.bitcast(sc, jnp.int32)
        key = bits ^ ((bits >> 31) & 0x7FFFFFFF)
        key = jnp.where(k0 + row <= t_pos, key, INT_MIN)
        key_s[pl.ds(k0, TK), :] = key
        return 0

    lax.fori_loop(0, nk, score_body, 0)

    def count(pred_fn):
        def body(kc, c):
            k0 = pl.multiple_of(kc * TK, TK)
            hit = pred_fn(key_s[pl.ds(k0, TK), :])
            return c + jnp.sum(jnp.where(hit, 1.0, 0.0), axis=0, keepdims=True)
        return lax.fori_loop(0, nk, body, jnp.zeros((1, TQ), F32))

    def bit_body(b, ans):
        cand = ans + jnp.left_shift(jnp.int32(1), 31 - b)
        cnt = count(lambda keys: keys >= cand)
        return jnp.where(cnt >= topk, cand, ans)

    thr = lax.fori_loop(0, 32, bit_body, jnp.full((1, TQ), INT_MIN, jnp.int32))
    need = topk - count(lambda keys: keys > thr)

    tri = (lax.broadcasted_iota(jnp.int32, (TK, TK), 1)
           < lax.broadcasted_iota(jnp.int32, (TK, TK), 0)).astype(BF16)

    def attn_body(kc, carry):
        eq_seen, soft = carry
        k0 = pl.multiple_of(kc * TK, TK)
        keys = key_s[pl.ds(k0, TK), :]
        eq = keys == thr
        eq_f = jnp.where(eq, 1.0, 0.0)
        before = eq_seen + jnp.dot(tri, eq_f.astype(BF16), preferred_element_type=F32)
        sel = (keys > thr) | (eq & (before < need))
        sel = sel & (k0 + row <= t_pos)
        bias = jnp.where(sel, 0.0, NEG)
        s = lax.dot_general(k_ref[0, pl.ds(k0, TK), :], q_all, _NT, preferred_element_type=F32)
        s = s + jnp.concatenate([bias] * N_TOK_HEADS, axis=1)
        soft = _online_step(s, v_ref[0, pl.ds(k0, TK), :], soft)
        return eq_seen + jnp.sum(eq_f, axis=0, keepdims=True), soft

    _, (m, l, acc) = lax.fori_loop(
        0, nk, attn_body, (jnp.zeros((1, TQ), F32), _online_init(N_TOK_HEADS * TQ)))
    _write_heads_t(out_ref, 0, acc / l, N_TOK_HEADS)
    _write_heads_t(out_ref, TOK_W, _mem_attention_t(qm_ref, km_ref, vm_ref), N_MEM_HEADS)


def _dsa_attn(q, qi, wt, kidx, k, v, qm, km, vm, B, S):
    NM = km.shape[2]
    topk = min(DSA_TOPK_MAX, S // 4)
    qblk = lambda nh: pl.BlockSpec((1, nh, TQ, HEAD_DIM), lambda b, i: (b, 0, i, 0))
    full = pl.BlockSpec((1, S, HEAD_DIM), lambda b, i: (b, 0, 0))
    memspec = pl.BlockSpec((1, N_MEM_HEADS, NM, HEAD_DIM), lambda b, i: (b, 0, 0, 0))
    return pl.pallas_call(
        functools.partial(_dsa_attn_kernel, topk=topk),
        grid=(B, S // TQ),
        in_specs=[
            qblk(N_TOK_HEADS), qblk(IDX_HEADS),
            pl.BlockSpec((1, IDX_HEADS, TQ), lambda b, i: (b, 0, i)),
            full, full, full,
            qblk(N_MEM_HEADS), memspec, memspec,
        ],
        out_specs=pl.BlockSpec((1, TQ, TOK_W + MEM_Q), lambda b, i: (b, i, 0)),
        out_shape=jax.ShapeDtypeStruct((B, S, TOK_W + MEM_Q), BF16),
        scratch_shapes=[pltpu.VMEM((S, TQ), jnp.int32)],
        compiler_params=_cparams(2),
        name="dsa_attn",
    )(q, qi, wt, kidx, k, v, qm, km, vm)


def _nsa_compress_kernel(kc_ref, vc_ref, pk_ref, pv_ref, kw1_ref, kw2_ref, vw1_ref, vw2_ref,
                         ko_ref, vo_ref):
    half = (CMP_LEN // 2) * HEAD_DIM

    def run(x_ref, pos_ref, w1_ref, w2_ref, o_ref):
        r = x_ref[0, 0]
        nxt = pltpu.roll(r, r.shape[0] - 1, 0)
        a = (r + pos_ref[0:1, :]).astype(BF16)
        b = (nxt + pos_ref[1:2, :]).astype(BF16)
        hid = (jnp.dot(a, w1_ref[:half, :], preferred_element_type=F32)
               + jnp.dot(b, w1_ref[half:, :], preferred_element_type=F32))
        hid = jax.nn.gelu(hid).astype(BF16)
        o_ref[0, 0] = jnp.dot(hid, w2_ref[...], preferred_element_type=F32).astype(o_ref.dtype)

    run(kc_ref, pk_ref, kw1_ref, kw2_ref, ko_ref)
    run(vc_ref, pv_ref, vw1_ref, vw2_ref, vo_ref)


def _nsa_compress(kc, vc, pos_k, pos_v, k_w1, k_w2, v_w1, v_w2, B, S):
    G = NSA_GROUPS
    R = S // CMP_STRIDE
    W = CMP_STRIDE * HEAD_DIM
    kc = kc.reshape(B, G, R, W)
    vc = vc.reshape(B, G, R, W)
    xs = pl.BlockSpec((1, 1, R, W), lambda b, g: (b, g, 0, 0))
    cst = lambda shp: pl.BlockSpec(shp, lambda b, g: (0, 0))
    osz = pl.BlockSpec((1, 1, R, HEAD_DIM), lambda b, g: (b, g, 0, 0))
    osh = jax.ShapeDtypeStruct((B, G, R, HEAD_DIM), BF16)
    return pl.pallas_call(
        _nsa_compress_kernel,
        grid=(B, G),
        in_specs=[xs, xs, cst((2, W)), cst((2, W)),
                  cst((CMP_LEN * HEAD_DIM, CMP_HIDDEN)), cst((CMP_HIDDEN, HEAD_DIM)),
                  cst((CMP_LEN * HEAD_DIM, CMP_HIDDEN)), cst((CMP_HIDDEN, HEAD_DIM))],
        out_specs=[osz, osz],
        out_shape=[osh, osh],
        compiler_params=_cparams(2),
        name="nsa_compress",
    )(kc, vc, pos_k.reshape(2, W), pos_v.reshape(2, W),
      k_w1.astype(BF16), k_w2.astype(BF16), v_w1.astype(BF16), v_w2.astype(BF16))


def _nsa_attn_kernel(qraw_ref, qrot_ref, kcmp_ref, vcmp_ref, ks_ref, vs_ref, kw_ref, vw_ref,
                     gt_ref, ovt_ref, qm_ref, km_ref, vm_ref, out_ref, sel_s,
                     *, n_cmp, n_slc, n_sel):
    i = pl.program_id(1)
    J = NSA_HPG
    L = J * TQ
    NC = kcmp_ref.shape[2]
    NSP = sel_s.shape[1]
    t_row = i * TQ + lax.broadcasted_iota(jnp.int32, (1, TQ), 1)
    t_pos = i * TQ + lax.broadcasted_iota(jnp.int32, (TK, TQ), 1)
    row = lax.broadcasted_iota(jnp.int32, (TK, TQ), 0)
    gt = gt_ref[0]

    c_idx = lax.broadcasted_iota(jnp.int32, (NC, TQ), 0)
    t_c = i * TQ + lax.broadcasted_iota(jnp.int32, (NC, TQ), 1)
    mask_c1 = (c_idx * CMP_STRIDE + CMP_LEN - 1 <= t_c) & (c_idx < n_cmp)
    mask_c = jnp.concatenate([mask_c1] * J, axis=1)

    n_idx = lax.broadcasted_iota(jnp.int32, (NSP, TQ), 0)
    cur = t_row // SLC_LEN
    forced = (n_idx == 0) | (n_idx == cur) | (n_idx == cur - 1)
    admissible = (n_idx <= cur) & (n_idx < n_slc)

    o_cs = []
    for g in range(NSA_GROUPS):
        h0 = g * J
        q_raw = qraw_ref[0, h0:h0 + J].reshape(L, HEAD_DIM)
        s_c = lax.dot_general(kcmp_ref[0, g], q_raw, _NT, preferred_element_type=F32)
        s_c = jnp.where(mask_c, s_c, -jnp.inf)
        m_c = jnp.max(s_c, axis=0, keepdims=True)
        m_c = jnp.where(m_c > -jnp.inf, m_c, 0.0)
        p_c = jnp.exp(s_c - m_c)
        p_c = p_c / jnp.maximum(jnp.sum(p_c, axis=0, keepdims=True), 1e-30)
        o_c = lax.dot_general(vcmp_ref[0, g], p_c.astype(BF16), _TN, preferred_element_type=F32)

        p_sum = p_c[:, 0:TQ]
        for j in range(1, J):
            p_sum = p_sum + p_c[:, j * TQ:(j + 1) * TQ]
        imp = jnp.dot(ovt_ref[...], p_sum, preferred_element_type=F32,
                      precision=lax.Precision.HIGHEST)
        imp = jnp.where(forced, FORCE_SCORE, imp)
        imp = jnp.where(admissible, imp, -jnp.inf)
        rank = jnp.zeros((NSP, TQ), F32)
        for mrow in range(n_slc):
            other = imp[mrow:mrow + 1, :]
            ahead = (other > imp) | ((other == imp) & (n_idx > mrow))
            rank = rank + jnp.where(ahead, 1.0, 0.0)
        sel_s[g] = jnp.where(rank < n_sel, 1.0, 0.0)
        o_cs.append(o_c)

    outs = []
    for g in range(NSA_GROUPS):
        h0 = g * J
        q_rot = qrot_ref[0, h0:h0 + J].reshape(L, HEAD_DIM)

        def sel_body(kc, soft, g=g, q_rot=q_rot):
            k0 = pl.multiple_of(kc * TK, TK)
            per = TK // SLC_LEN
            rows = [jnp.broadcast_to(sel_s[g, pl.ds(kc * per + r, 1), :], (SLC_LEN, TQ))
                    for r in range(per)]
            picked = jnp.concatenate(rows, axis=0) > 0.5
            ok = picked & (k0 + row <= t_pos)
            bias = jnp.where(ok, 0.0, NEG)
            s = lax.dot_general(ks_ref[0, g, pl.ds(k0, TK), :], q_rot, _NT, preferred_element_type=F32)
            s = s + jnp.concatenate([bias] * J, axis=1)
            return _online_step(s, vs_ref[0, g, pl.ds(k0, TK), :], soft)

        _, l_s, acc_s = lax.fori_loop(0, i + 1, sel_body, _online_init(L))
        o_s = acc_s / l_s

        def win_body(kc, soft, g=g, q_rot=q_rot):
            k0 = pl.multiple_of(kc * TK, TK)
            kpos = k0 + row
            ok = (kpos <= t_pos) & (kpos > t_pos - WIN)
            bias = jnp.where(ok, 0.0, NEG)
            s = lax.dot_general(kw_ref[0, g, pl.ds(k0, TK), :], q_rot, _NT, preferred_element_type=F32)
            s = s + jnp.concatenate([bias] * J, axis=1)
            return _online_step(s, vw_ref[0, g, pl.ds(k0, TK), :], soft)

        lo = jnp.maximum(i - WIN // TK, 0)
        _, l_w, acc_w = lax.fori_loop(lo, i + 1, win_body, _online_init(L))
        o_w = acc_w / l_w

        o_c = o_cs[g]

        def gate_row(r):
            return jnp.concatenate([gt[3 * (h0 + j) + r:3 * (h0 + j) + r + 1, :] for j in range(J)], axis=1)

        outs.append(gate_row(0) * o_c + gate_row(1) * o_s + gate_row(2) * o_w)

    _write_heads_t(out_ref, 0, jnp.concatenate(outs, axis=1), N_TOK_HEADS)
    _write_heads_t(out_ref, TOK_W, _mem_attention_t(qm_ref, km_ref, vm_ref), N_MEM_HEADS)


def _nsa_attn(qraw, qrot, kcmp, vcmp, ks, vs, kw, vw, gt, qm, km, vm, B, S):
    G = NSA_GROUPS
    NM = km.shape[2]
    NC = kcmp.shape[2]
    n_cmp = (S - CMP_LEN) // CMP_STRIDE + 1
    n_slc = S // SLC_LEN
    n_sel = min(SLC_TOP_MAX, n_slc)
    nsp = -(-n_slc // 8) * 8
    c0 = np.arange(NC) * CMP_STRIDE
    s0 = np.arange(nsp) * SLC_LEN
    ov = np.minimum(c0[None, :] + CMP_LEN, s0[:, None] + SLC_LEN) - np.maximum(c0[None, :], s0[:, None])
    ovt = (np.clip(ov, 0, None) / CMP_LEN).astype(np.float32)
    ovt[:, n_cmp:] = 0.0
    ovt[n_slc:, :] = 0.0

    qblk = lambda nh: pl.BlockSpec((1, nh, TQ, HEAD_DIM), lambda b, i: (b, 0, i, 0))
    full = pl.BlockSpec((1, G, S, HEAD_DIM), lambda b, i: (b, 0, 0, 0))
    cmpspec = pl.BlockSpec((1, G, NC, HEAD_DIM), lambda b, i: (b, 0, 0, 0))
    memspec = pl.BlockSpec((1, N_MEM_HEADS, NM, HEAD_DIM), lambda b, i: (b, 0, 0, 0))
    return pl.pallas_call(
        functools.partial(_nsa_attn_kernel, n_cmp=n_cmp, n_slc=n_slc, n_sel=n_sel),
        grid=(B, S // TQ),
        in_specs=[
            qblk(N_TOK_HEADS), qblk(N_TOK_HEADS), cmpspec, cmpspec, full, full, full, full,
            pl.BlockSpec((1, GATE_ROWS, TQ), lambda b, i: (b, 0, i)),
            pl.BlockSpec((nsp, NC), lambda b, i: (0, 0)),
            qblk(N_MEM_HEADS), memspec, memspec,
        ],
        out_specs=pl.BlockSpec((1, TQ, TOK_W + MEM_Q), lambda b, i: (b, i, 0)),
        out_shape=jax.ShapeDtypeStruct((B, S, TOK_W + MEM_Q), BF16),
        scratch_shapes=[pltpu.VMEM((G, nsp, TQ), F32)],
        compiler_params=_cparams(2),
        name="nsa_attn",
    )(qraw, qrot, kcmp, vcmp, ks, vs, kw, vw, gt, jnp.asarray(ovt), qm, km, vm)


def _post_attn_kernel(x_ref, mix_ref, wo_ref, g_ref, win_ref, wdn_ref, gf_ref, out_ref, *, final):
    d_ff = wdn_ref.shape[0]
    x1 = x_ref[...] + jnp.dot(mix_ref[...], wo_ref[...], preferred_element_type=F32)
    h = _rms(x1, g_ref[...]).astype(BF16)
    gate = jnp.dot(h, win_ref[:, :d_ff], preferred_element_type=F32)
    up = jnp.dot(h, win_ref[:, d_ff:], preferred_element_type=F32)
    act = (jax.nn.silu(gate) * up).astype(BF16)
    x2 = x1 + jnp.dot(act, wdn_ref[...], preferred_element_type=F32)
    if final:
        x2 = _rms(x2, gf_ref[...])
    out_ref[...] = x2


def _post_attn(x2d, mix2d, w_o, g, w_in, w_down, g_final, final):
    N, D = x2d.shape
    MW = mix2d.shape[1]
    d_ff = w_down.shape[0]
    tm = min(TM_FFN, N)
    const = lambda shp: pl.BlockSpec(shp, lambda i: (0, 0), pipeline_mode=pl.Buffered(1))
    return pl.pallas_call(
        functools.partial(_post_attn_kernel, final=final),
        grid=(N // tm,),
        in_specs=[
            pl.BlockSpec((tm, D), lambda i: (i, 0)),
            pl.BlockSpec((tm, MW), lambda i: (i, 0)),
            const((MW, D)), const((1, D)), const((D, 2 * d_ff)), const((d_ff, D)), const((1, D)),
        ],
        out_specs=pl.BlockSpec((tm, D), lambda i: (i, 0)),
        out_shape=jax.ShapeDtypeStruct((N, D), F32),
        compiler_params=_cparams(1),
        name="post_attn",
    )(x2d, mix2d, w_o.astype(BF16), g.reshape(1, D), w_in.astype(BF16), w_down.astype(BF16),
      g_final.reshape(1, D))


def kernel(x, mem, attn_norm, mem_norm, ffn_norm, final_norm, dsa_w_in, dsa_ckv_norm, dsa_w_uk, dsa_w_uv, nsa_w_in, nsa_cmp_pos_k, nsa_cmp_pos_v, nsa_cmp_k_w1, nsa_cmp_k_w2, nsa_cmp_v_w1, nsa_cmp_v_w2, mem_w_kv, w_o, ffn_w_in, ffn_w_down):
    B, S, D = x.shape
    depth = attn_norm.shape[0]
    assert S % TM_PROJ == 0 or S < TM_PROJ
    assert S % TQ == 0 and (B * S) % TM_FFN == 0
    tabs = _rope_tables(S)
    x2d = x.reshape(B * S, D)
    for i in range(depth):
        km, vm = _mem_kv(mem, mem_norm[i], mem_w_kv[i])
        if i % 2 == 0:
            a = i // 2
            q, qi, kidx, k, v, wt, qm = _proj_dsa(
                x2d, attn_norm[i], dsa_w_in[a], dsa_ckv_norm[a], dsa_w_uk[a], dsa_w_uv[a], tabs, B, S)
            mix = _dsa_attn(q, qi, wt, kidx, k, v, qm, km, vm, B, S)
        else:
            b = i // 2
            qraw, qrot, kc, vc, ks, vs, kw, vw, gt, qm = _proj_nsa(
                x2d, attn_norm[i], nsa_w_in[b], tabs, B, S)
            kcmp, vcmp = _nsa_compress(
                kc, vc, nsa_cmp_pos_k[b], nsa_cmp_pos_v[b],
                nsa_cmp_k_w1[b], nsa_cmp_k_w2[b], nsa_cmp_v_w1[b], nsa_cmp_v_w2[b], B, S)
            mix = _nsa_attn(qraw, qrot, kcmp, vcmp, ks, vs, kw, vw, gt, qm, km, vm, B, S)
        x2d = _post_attn(x2d, mix.reshape(B * S, TOK_W + MEM_Q), w_o[i], ffn_norm[i],
                         ffn_w_in[i], ffn_w_down[i], final_norm, final=(i == depth - 1))
    return x2d.reshape(B, S, D)
```

```python
import functools
import math

import numpy as np
import jax
import jax.numpy as jnp
from jax import lax
from jax.experimental import pallas as pl
from jax.experimental.pallas import tpu as pltpu

F32 = jnp.float32
BF16 = jnp.bfloat16

HEAD_DIM = 64
ROPE_DIM = 16
ROPE_THETA = 500000.0
RMS_EPS = 1e-6
N_TOK_HEADS = 12
N_MEM_HEADS = 4
TOK_W = N_TOK_HEADS * HEAD_DIM
MEM_Q = N_MEM_HEADS * HEAD_DIM
DSA_KV_RANK = 128
DSA_NOPE = HEAD_DIM - ROPE_DIM
IDX_HEADS = 8
IDX_DIM = 64
DSA_TOPK_MAX = 256
NSA_GROUPS = 2
NSA_HPG = N_TOK_HEADS // NSA_GROUPS
CMP_LEN = 32
CMP_STRIDE = 16
CMP_HIDDEN = 128
SLC_LEN = 64
SLC_TOP_MAX = 16
WIN = 512
FORCE_SCORE = 1e9
N_GATES = N_TOK_HEADS * 3
GATE_ROWS = 40

LANE = 128
TQ = 128
TK = 256
TM_PROJ = 512
TM_FFN = 256
VMEM_LIMIT = 56 * 1024 * 1024
NEG = -1e30
INT_MIN = -2 ** 31

_NT = (((1,), (1,)), ((), ()))
_TN = (((0,), (0,)), ((), ()))


def _cparams(n_axes):
    return pltpu.CompilerParams(
        dimension_semantics=("arbitrary",) * n_axes, vmem_limit_bytes=VMEM_LIMIT)


def _rms(xf, g):
    ms = jnp.mean(xf * xf, axis=-1, keepdims=True)
    return xf * lax.rsqrt(ms + RMS_EPS) * g


def _rope_tables(S):
    inv = ROPE_THETA ** (-np.arange(0, ROPE_DIM, 2, dtype=np.float64) / ROPE_DIM)
    ang = np.arange(S, dtype=np.float64)[:, None] * inv[None, :]
    cos = np.cos(ang).astype(np.float32)
    sin = np.sin(ang).astype(np.float32)
    half = ROPE_DIM // 2
    c = np.ones((S, LANE), np.float32)
    s_up = np.zeros((S, LANE), np.float32)
    s_dn = np.zeros((S, LANE), np.float32)
    for base in range(0, LANE, HEAD_DIM):
        c[:, base:base + half] = cos
        c[:, base + half:base + ROPE_DIM] = cos
        s_up[:, base + half:base + ROPE_DIM] = sin
        s_dn[:, base:base + half] = -sin
    return jnp.asarray(c), jnp.asarray(s_up), jnp.asarray(s_dn)


def _rope(x, c, s_up, s_dn):
    half = ROPE_DIM // 2
    outs = []
    for j in range(x.shape[1] // LANE):
        xc = x[:, j * LANE:(j + 1) * LANE]
        outs.append(xc * c + pltpu.roll(xc, half, 1) * s_up + pltpu.roll(xc, LANE - half, 1) * s_dn)
    return outs


def _store_heads(ref, chunks, dtype):
    for j, ch in enumerate(chunks):
        ref[0, 2 * j, :, :] = ch[:, :HEAD_DIM].astype(dtype)
        ref[0, 2 * j + 1, :, :] = ch[:, HEAD_DIM:].astype(dtype)


def _chunks(x):
    return [x[:, j * LANE:(j + 1) * LANE] for j in range(x.shape[1] // LANE)]


def _store_chunks_t(ref, x, row_lo, row_hi):
    xt = x.T
    for j in range(x.shape[0] // TK):
        ref[0, j] = xt[row_lo:row_hi, j * TK:(j + 1) * TK].astype(ref.dtype)


def _proj_dsa_kernel(x_ref, g_ref, w_ref, gkv_ref, wkv_ref, c_ref, su_ref, sd_ref,
                     q_ref, qi_ref, kidx_ref, k_ref, vt_ref, wt_ref, qm_ref):
    h = _rms(x_ref[...], g_ref[...]).astype(BF16)
    c, su, sd = c_ref[...], su_ref[...], sd_ref[...]

    def mm(lo, hi):
        return jnp.dot(h, w_ref[:, lo:hi], preferred_element_type=F32)

    scale = HEAD_DIM ** -0.5
    q = _rope(mm(0, TOK_W), c, su, sd)
    _store_heads(q_ref, [ch * scale for ch in q], BF16)

    ckv = _rms(mm(TOK_W, TOK_W + DSA_KV_RANK), gkv_ref[...]).astype(BF16)
    kvn = jnp.dot(ckv, wkv_ref[...], preferred_element_type=F32)

    o_qi = TOK_W + DSA_KV_RANK
    qi = _rope(mm(o_qi, o_qi + IDX_HEADS * IDX_DIM), c, su, sd)
    _store_heads(qi_ref, qi, BF16)

    o_misc = o_qi + IDX_HEADS * IDX_DIM
    misc = _rope(mm(o_misc, o_misc + LANE), c, su, sd)[0]
    kidx_ref[0] = misc[:, :IDX_DIM].astype(BF16)
    lane = lax.broadcasted_iota(jnp.int32, misc.shape, 1)
    k_full = jnp.where(lane < ROPE_DIM, pltpu.roll(misc, HEAD_DIM, 1), kvn)
    k_ref[0] = k_full[:, :HEAD_DIM].astype(BF16)
    _store_chunks_t(vt_ref, kvn, HEAD_DIM, LANE)
    w_scaled = misc * (IDX_HEADS ** -0.5 * IDX_DIM ** -0.5)
    w_row0 = IDX_DIM + ROPE_DIM
    wt_ref[0] = w_scaled.T[w_row0:w_row0 + IDX_HEADS, :]

    o_qm = o_misc + LANE
    qm = mm(o_qm, o_qm + MEM_Q)
    _store_heads(qm_ref, [ch * scale for ch in _chunks(qm)], BF16)


def _proj_dsa(x2d, g, w_in, gkv, w_uk, w_uv, tabs, B, S):
    D = x2d.shape[1]
    o = np.cumsum([0, TOK_W, DSA_KV_RANK, ROPE_DIM, IDX_HEADS * IDX_DIM, IDX_DIM, IDX_HEADS, MEM_Q])
    q, ckv, kr, qi, ki, wi, qm = [w_in[:, o[i]:o[i + 1]] for i in range(7)]
    pad = jnp.zeros((D, LANE - IDX_DIM - ROPE_DIM - IDX_HEADS), w_in.dtype)
    w = jnp.concatenate([q, ckv, qi, ki, kr, wi, pad, qm], axis=1).astype(BF16)
    wkv = jnp.concatenate([jnp.zeros((DSA_KV_RANK, ROPE_DIM), F32), w_uk, w_uv], axis=1).astype(BF16)
    ncol = w.shape[1]
    tm = min(TM_PROJ, S)
    nst = S // tm
    tok = lambda i: (i, 0)
    const = lambda i: (0, 0)
    tab = lambda i: (i % nst, 0)
    hm = lambda i: (i // nst, 0, i % nst, 0)
    row = lambda i: (i // nst, i % nst, 0)
    outs = pl.pallas_call(
        _proj_dsa_kernel,
        grid=(B * nst,),
        in_specs=[
            pl.BlockSpec((tm, D), tok),
            pl.BlockSpec((1, D), const),
            pl.BlockSpec((D, ncol), const),
            pl.BlockSpec((1, DSA_KV_RANK), const),
            pl.BlockSpec((DSA_KV_RANK, LANE), const),
            pl.BlockSpec((tm, LANE), tab),
            pl.BlockSpec((tm, LANE), tab),
            pl.BlockSpec((tm, LANE), tab),
        ],
        out_specs=[
            pl.BlockSpec((1, N_TOK_HEADS, tm, HEAD_DIM), hm),
            pl.BlockSpec((1, IDX_HEADS, tm, IDX_DIM), hm),
            pl.BlockSpec((1, tm, IDX_DIM), row),
            pl.BlockSpec((1, tm, HEAD_DIM), row),
            pl.BlockSpec((1, tm // TK, HEAD_DIM, TK), lambda i: (i // nst, i % nst, 0, 0)),
            pl.BlockSpec((1, IDX_HEADS, tm), lambda i: (i // nst, 0, i % nst)),
            pl.BlockSpec((1, N_MEM_HEADS, tm, HEAD_DIM), hm),
        ],
        out_shape=[
            jax.ShapeDtypeStruct((B, N_TOK_HEADS, S, HEAD_DIM), BF16),
            jax.ShapeDtypeStruct((B, IDX_HEADS, S, IDX_DIM), BF16),
            jax.ShapeDtypeStruct((B, S, IDX_DIM), BF16),
            jax.ShapeDtypeStruct((B, S, HEAD_DIM), BF16),
            jax.ShapeDtypeStruct((B, S // TK, HEAD_DIM, TK), BF16),
            jax.ShapeDtypeStruct((B, IDX_HEADS, S), F32),
            jax.ShapeDtypeStruct((B, N_MEM_HEADS, S, HEAD_DIM), BF16),
        ],
        compiler_params=_cparams(1),
        name="proj_dsa",
    )(x2d, g.reshape(1, D), w, gkv.reshape(1, DSA_KV_RANK), wkv, *tabs)
    return outs


def _proj_nsa_kernel(x_ref, g_ref, w_ref, c_ref, su_ref, sd_ref,
                     qraw_ref, qrot_ref, kc_ref, vc_ref, ks_ref, vs_ref, kw_ref, vw_ref,
                     gt_ref, qm_ref):
    h = _rms(x_ref[...], g_ref[...]).astype(BF16)
    c, su, sd = c_ref[...], su_ref[...], sd_ref[...]

    def mm(lo, hi):
        return jnp.dot(h, w_ref[:, lo:hi], preferred_element_type=F32)

    scale = HEAD_DIM ** -0.5
    q = mm(0, TOK_W)
    _store_heads(qraw_ref, [ch * scale for ch in _chunks(q)], BF16)
    _store_heads(qrot_ref, [ch * scale for ch in _rope(q, c, su, sd)], BF16)

    o = TOK_W
    _store_heads(kc_ref, _chunks(mm(o, o + LANE)), F32)
    _store_heads(vc_ref, _chunks(mm(o + LANE, o + 2 * LANE)), F32)
    _store_heads(ks_ref, _rope(mm(o + 2 * LANE, o + 3 * LANE), c, su, sd), BF16)
    _store_chunks_t(vs_ref, mm(o + 3 * LANE, o + 4 * LANE), 0, LANE)
    _store_heads(kw_ref, _rope(mm(o + 4 * LANE, o + 5 * LANE), c, su, sd), BF16)
    _store_chunks_t(vw_ref, mm(o + 5 * LANE, o + 6 * LANE), 0, LANE)

    gates = jax.nn.sigmoid(mm(o + 6 * LANE, o + 7 * LANE))
    gt_ref[0] = gates.T[:GATE_ROWS, :]

    qm = mm(o + 7 * LANE, o + 7 * LANE + MEM_Q)
    _store_heads(qm_ref, [ch * scale for ch in _chunks(qm)], BF16)


def _proj_nsa(x2d, g, w_in, tabs, B, S):
    D = x2d.shape[1]
    kv_w = NSA_GROUPS * HEAD_DIM
    o_g = TOK_W + 6 * kv_w
    pad = jnp.zeros((D, LANE - N_GATES), w_in.dtype)
    w = jnp.concatenate([w_in[:, :o_g + N_GATES], pad, w_in[:, o_g + N_GATES:]], axis=1).astype(BF16)
    ncol = w.shape[1]
    tm = min(TM_PROJ, S)
    nst = S // tm
    tok = lambda i: (i, 0)
    const = lambda i: (0, 0)
    tab = lambda i: (i % nst, 0)
    hm = lambda i: (i // nst, 0, i % nst, 0)

    def hm_spec(nh):
        return pl.BlockSpec((1, nh, tm, HEAD_DIM), hm)

    def hm_shape(nh, dt):
        return jax.ShapeDtypeStruct((B, nh, S, HEAD_DIM), dt)

    G = NSA_GROUPS
    vt_spec = pl.BlockSpec((1, tm // TK, G * HEAD_DIM, TK), lambda i: (i // nst, i % nst, 0, 0))
    vt_shape = jax.ShapeDtypeStruct((B, S // TK, G * HEAD_DIM, TK), BF16)
    outs = pl.pallas_call(
        _proj_nsa_kernel,
        grid=(B * nst,),
        in_specs=[
            pl.BlockSpec((tm, D), tok),
            pl.BlockSpec((1, D), const),
            pl.BlockSpec((D, ncol), const),
            pl.BlockSpec((tm, LANE), tab),
            pl.BlockSpec((tm, LANE), tab),
            pl.BlockSpec((tm, LANE), tab),
        ],
        out_specs=[
            hm_spec(N_TOK_HEADS), hm_spec(N_TOK_HEADS),
            hm_spec(G), hm_spec(G), hm_spec(G), vt_spec, hm_spec(G), vt_spec,
            pl.BlockSpec((1, GATE_ROWS, tm), lambda i: (i // nst, 0, i % nst)),
            hm_spec(N_MEM_HEADS),
        ],
        out_shape=[
            hm_shape(N_TOK_HEADS, BF16), hm_shape(N_TOK_HEADS, BF16),
            hm_shape(G, F32), hm_shape(G, F32),
            hm_shape(G, BF16), vt_shape, hm_shape(G, BF16), vt_shape,
            jax.ShapeDtypeStruct((B, GATE_ROWS, S), F32),
            hm_shape(N_MEM_HEADS, BF16),
        ],
        compiler_params=_cparams(1),
        name="proj_nsa",
    )(x2d, g.reshape(1, D), w, *tabs)
    return outs


def _mem_kv_kernel(m_ref, g_ref, w_ref, k_ref, vt_ref):
    h = _rms(m_ref[0], g_ref[...]).astype(BF16)
    kv = jnp.dot(h, w_ref[...], preferred_element_type=F32)
    _store_heads(k_ref, _chunks(kv[:, :MEM_Q]), BF16)
    for j in range(MEM_Q // LANE):
        vt_ref[0, j * LANE:(j + 1) * LANE, :] = kv[:, MEM_Q + j * LANE:MEM_Q + (j + 1) * LANE].T.astype(BF16)


def _mem_kv(mem, g, w_kv):
    B, NM, D = mem.shape
    assert NM % LANE == 0
    spec = pl.BlockSpec((1, N_MEM_HEADS, NM, HEAD_DIM), lambda b: (b, 0, 0, 0))
    shape = jax.ShapeDtypeStruct((B, N_MEM_HEADS, NM, HEAD_DIM), BF16)
    vt_spec = pl.BlockSpec((1, MEM_Q, NM), lambda b: (b, 0, 0))
    vt_shape = jax.ShapeDtypeStruct((B, MEM_Q, NM), BF16)
    return pl.pallas_call(
        _mem_kv_kernel,
        grid=(B,),
        in_specs=[
            pl.BlockSpec((1, NM, D), lambda b: (b, 0, 0)),
            pl.BlockSpec((1, D), lambda b: (0, 0)),
            pl.BlockSpec((D, 2 * MEM_Q), lambda b: (0, 0)),
        ],
        out_specs=[spec, vt_spec],
        out_shape=[shape, vt_shape],
        compiler_params=_cparams(1),
        name="mem_kv",
    )(mem, g.reshape(1, D), w_kv.astype(BF16))


def _write_pair_t(out_ref, col, o_a, o_b):
    pair = jnp.concatenate([o_a, o_b], axis=0)
    out_ref[0, :, col:col + 2 * HEAD_DIM] = pair.T.astype(out_ref.dtype)


def _mem_attention_t(qm_ref, km_ref, vmt_ref, out_ref):
    outs = []
    for h in range(N_MEM_HEADS):
        s = lax.dot_general(km_ref[0, h], qm_ref[0, h], _NT, preferred_element_type=F32)
        m = jnp.max(s, axis=0, keepdims=True)
        p = jnp.exp(s - m)
        l = jnp.sum(p, axis=0, keepdims=True)
        vt = vmt_ref[0, h * HEAD_DIM:(h + 1) * HEAD_DIM, :]
        outs.append(jnp.dot(vt, p.astype(BF16), preferred_element_type=F32) / l)
    for h in range(0, N_MEM_HEADS, 2):
        _write_pair_t(out_ref, TOK_W + h * HEAD_DIM, outs[h], outs[h + 1])


def _attend_groups(groups, q_ref, lo, hi, m_s, l_s, acc_s):
    heads = [h for (h0, nh, _, _, _) in groups for h in range(h0, h0 + nh)]
    for h in heads:
        m_s[h] = jnp.full((1, TQ), NEG, F32)
        l_s[h] = jnp.zeros((1, TQ), F32)
        acc_s[h] = jnp.zeros((HEAD_DIM, TQ), F32)

    def body(kc, _):
        scores = []
        for (h0, nh, k_chunk, _, bias_chunk) in groups:
            q_g = q_ref[0, h0:h0 + nh].reshape(nh * TQ, HEAD_DIM)
            scores.append(lax.dot_general(k_chunk(kc), q_g, _NT, preferred_element_type=F32))
        for (h0, nh, _, vt_chunk, bias_chunk), s_g in zip(groups, scores):
            bias = bias_chunk(kc)
            probs, alphas = [], []
            for j in range(nh):
                h = h0 + j
                s = s_g[:, j * TQ:(j + 1) * TQ] + bias
                m = m_s[h]
                m_new = jnp.maximum(m, jnp.max(s, axis=0, keepdims=True))
                alpha = jnp.exp(m - m_new)
                p = jnp.exp(s - m_new)
                m_s[h] = m_new
                l_s[h] = alpha * l_s[h] + jnp.sum(p, axis=0, keepdims=True)
                probs.append(p.astype(BF16))
                alphas.append(alpha)
            pv = jnp.dot(vt_chunk(kc), jnp.concatenate(probs, axis=1), preferred_element_type=F32)
            for j in range(nh):
                acc_s[h0 + j] = alphas[j] * acc_s[h0 + j] + pv[:, j * TQ:(j + 1) * TQ]
        return 0

    lax.fori_loop(lo, hi, body, 0)
    for h in heads:
        acc_s[h] = acc_s[h] / l_s[h]


def _dsa_attn_kernel(q_ref, qi_ref, wt_ref, kidx_ref, k_ref, vt_ref, qm_ref, km_ref, vmt_ref,
                     out_ref, key_s, bias_s, m_s, l_s, o_s, *, topk):
    i = pl.program_id(1)
    nk = ((i + 1) * TQ + TK - 1) // TK
    t_pos = i * TQ + lax.broadcasted_iota(jnp.int32, (TK, TQ), 1)
    row = lax.broadcasted_iota(jnp.int32, (TK, TQ), 0)
    wt = wt_ref[0]

    def rows(kc):
        return pl.ds(pl.multiple_of(kc * TK, TK), TK)

    def score_body(kc, _):
        kx = kidx_ref[0, rows(kc), :]
        sc = jnp.zeros((TK, TQ), F32)
        for h in range(IDX_HEADS):
            lg = lax.dot_general(kx, qi_ref[0, h], _NT, preferred_element_type=F32)
            sc = sc + jnp.maximum(lg, 0.0) * wt[h:h + 1, :]
        sc = jnp.where(sc == 0.0, 0.0, sc)
        bits = pltpu.bitcast(sc, jnp.int32)
        key = bits ^ ((bits >> 31) & 0x7FFFFFFF)
        key_s[rows(kc), :] = jnp.where(kc * TK + row <= t_pos, key, INT_MIN)
        return 0

    lax.fori_loop(0, nk, score_body, 0)

    n_acc = 4 * 8

    def count(pred_fn):
        def body(kc, c):
            hit = jnp.where(pred_fn(key_s[rows(kc), :]), 1.0, 0.0)
            return c + jnp.sum(hit.reshape(TK // n_acc, n_acc, TQ), axis=0)

        c = lax.fori_loop(0, nk, body, jnp.zeros((n_acc, TQ), F32))
        return jnp.sum(c, axis=0, keepdims=True)

    def bit_body(b, carry):
        ans, n_ge = carry
        cand = ans + jnp.left_shift(jnp.int32(1), 31 - b)
        cnt = count(lambda keys: keys >= cand)
        take = cnt >= topk
        return jnp.where(take, cand, ans), jnp.where(take, cnt, n_ge)

    thr, n_ge = lax.fori_loop(
        0, 32, bit_body, (jnp.full((1, TQ), INT_MIN, jnp.int32), jnp.full((1, TQ), float(topk), F32)))
    has_ties = jnp.max(jnp.where((n_ge > topk) & (thr > INT_MIN), 1.0, 0.0)) > 0.0

    @pl.when(jnp.logical_not(has_ties))
    def _():
        def bias_body(kc, _):
            sel = (key_s[rows(kc), :] >= thr) & (kc * TK + row <= t_pos)
            bias_s[rows(kc), :] = jnp.where(sel, 0.0, NEG)
            return 0

        lax.fori_loop(0, nk, bias_body, 0)

    @pl.when(has_ties)
    def _():
        need = topk - count(lambda keys: keys > thr)
        tri = (lax.broadcasted_iota(jnp.int32, (TK, TK), 1)
               < lax.broadcasted_iota(jnp.int32, (TK, TK), 0)).astype(BF16)

        def bias_body(kc, eq_seen):
            keys = key_s[rows(kc), :]
            eq = keys == thr
            eq_f = jnp.where(eq, 1.0, 0.0)
            before = eq_seen + jnp.dot(tri, eq_f.astype(BF16), preferred_element_type=F32)
            sel = (keys > thr) | (eq & (before < need))
            sel = sel & (kc * TK + row <= t_pos)
            bias_s[rows(kc), :] = jnp.where(sel, 0.0, NEG)
            return eq_seen + jnp.sum(eq_f, axis=0, keepdims=True)

        lax.fori_loop(0, nk, bias_body, jnp.zeros((1, TQ), F32))

    _attend_groups(
        [(0, N_TOK_HEADS,
          lambda kc: k_ref[0, rows(kc), :],
          lambda kc: vt_ref[0, kc],
          lambda kc: bias_s[rows(kc), :])],
        q_ref, 0, nk, m_s, l_s, o_s)
    for h in range(0, N_TOK_HEADS, 2):
        _write_pair_t(out_ref, h * HEAD_DIM, o_s[h], o_s[h + 1])
    _mem_attention_t(qm_ref, km_ref, vmt_ref, out_ref)


def _dsa_attn(q, qi, wt, kidx, k, vt, qm, km, vmt, B, S):
    NM = km.shape[2]
    topk = min(DSA_TOPK_MAX, S // 4)
    assert S % TK == 0
    key_rows = S
    stat = pltpu.VMEM((N_TOK_HEADS, 1, TQ), F32)
    qblk = lambda nh: pl.BlockSpec((1, nh, TQ, HEAD_DIM), lambda b, i: (b, 0, i, 0))
    full = pl.BlockSpec((1, S, HEAD_DIM), lambda b, i: (b, 0, 0))
    return pl.pallas_call(
        functools.partial(_dsa_attn_kernel, topk=topk),
        grid=(B, S // TQ),
        in_specs=[
            qblk(N_TOK_HEADS), qblk(IDX_HEADS),
            pl.BlockSpec((1, IDX_HEADS, TQ), lambda b, i: (b, 0, i)),
            full, full,
            pl.BlockSpec((1, S // TK, HEAD_DIM, TK), lambda b, i: (b, 0, 0, 0)),
            qblk(N_MEM_HEADS),
            pl.BlockSpec((1, N_MEM_HEADS, NM, HEAD_DIM), lambda b, i: (b, 0, 0, 0)),
            pl.BlockSpec((1, MEM_Q, NM), lambda b, i: (b, 0, 0)),
        ],
        out_specs=pl.BlockSpec((1, TQ, TOK_W + MEM_Q), lambda b, i: (b, i, 0)),
        out_shape=jax.ShapeDtypeStruct((B, S, TOK_W + MEM_Q), BF16),
        scratch_shapes=[
            pltpu.VMEM((key_rows, TQ), jnp.int32),
            pltpu.VMEM((S, TQ), F32),
            stat, stat,
            pltpu.VMEM((N_TOK_HEADS, HEAD_DIM, TQ), F32),
        ],
        compiler_params=_cparams(2),
        name="dsa_attn",
    )(q, qi, wt, kidx, k, vt, qm, km, vmt)


def _nsa_compress_kernel(kc_ref, vc_ref, pk_ref, pv_ref, kw1_ref, kw2_ref, vw1_ref, vw2_ref,
                         ko_ref, vo_ref):
    half = (CMP_LEN // 2) * HEAD_DIM

    def run(x_ref, pos_ref, w1_ref, w2_ref, o_ref):
        r = x_ref[0, 0]
        nxt = pltpu.roll(r, r.shape[0] - 1, 0)
        a = (r + pos_ref[0:1, :]).astype(BF16)
        b = (nxt + pos_ref[1:2, :]).astype(BF16)
        hid = (jnp.dot(a, w1_ref[:half, :], preferred_element_type=F32)
               + jnp.dot(b, w1_ref[half:, :], preferred_element_type=F32))
        hid = jax.nn.gelu(hid).astype(BF16)
        o_ref[0, 0] = jnp.dot(hid, w2_ref[...], preferred_element_type=F32).astype(o_ref.dtype)

    run(kc_ref, pk_ref, kw1_ref, kw2_ref, ko_ref)
    run(vc_ref, pv_ref, vw1_ref, vw2_ref, vo_ref)


def _nsa_compress(kc, vc, pos_k, pos_v, k_w1, k_w2, v_w1, v_w2, B, S):
    G = NSA_GROUPS
    R = S // CMP_STRIDE
    W = CMP_STRIDE * HEAD_DIM
    kc = kc.reshape(B, G, R, W)
    vc = vc.reshape(B, G, R, W)
    xs = pl.BlockSpec((1, 1, R, W), lambda b, g: (b, g, 0, 0))
    cst = lambda shp: pl.BlockSpec(shp, lambda b, g: (0, 0))
    osz = pl.BlockSpec((1, 1, R, HEAD_DIM), lambda b, g: (b, g, 0, 0))
    osh = jax.ShapeDtypeStruct((B, G, R, HEAD_DIM), BF16)
    return pl.pallas_call(
        _nsa_compress_kernel,
        grid=(B, G),
        in_specs=[xs, xs, cst((2, W)), cst((2, W)),
                  cst((CMP_LEN * HEAD_DIM, CMP_HIDDEN)), cst((CMP_HIDDEN, HEAD_DIM)),
                  cst((CMP_LEN * HEAD_DIM, CMP_HIDDEN)), cst((CMP_HIDDEN, HEAD_DIM))],
        out_specs=[osz, osz],
        out_shape=[osh, osh],
        compiler_params=_cparams(2),
        name="nsa_compress",
    )(kc, vc, pos_k.reshape(2, W), pos_v.reshape(2, W),
      k_w1.astype(BF16), k_w2.astype(BF16), v_w1.astype(BF16), v_w2.astype(BF16))


def _nsa_attn_kernel(qraw_ref, qrot_ref, kcmp_ref, vcmp_ref, ks_ref, vst_ref, kw_ref, vwt_ref,
                     gt_ref, ovt_ref, qm_ref, km_ref, vmt_ref, out_ref,
                     sel_s, bsel_s, wb_s, m_s, l_s, oc_s, os_s, ow_s, *, n_cmp, n_slc, n_sel):
    i = pl.program_id(1)
    J = NSA_HPG
    L = J * TQ
    NC = kcmp_ref.shape[2]
    NSP = sel_s.shape[1]
    t_row = i * TQ + lax.broadcasted_iota(jnp.int32, (1, TQ), 1)
    t_pos = i * TQ + lax.broadcasted_iota(jnp.int32, (TK, TQ), 1)
    row = lax.broadcasted_iota(jnp.int32, (TK, TQ), 0)
    gt = gt_ref[0]

    c_idx = lax.broadcasted_iota(jnp.int32, (NC, TQ), 0)
    t_c = i * TQ + lax.broadcasted_iota(jnp.int32, (NC, TQ), 1)
    mask_c1 = (c_idx * CMP_STRIDE + CMP_LEN - 1 <= t_c) & (c_idx < n_cmp)
    mask_c = jnp.concatenate([mask_c1] * J, axis=1)

    n_idx = lax.broadcasted_iota(jnp.int32, (NSP, TQ), 0)
    cur = t_row // SLC_LEN
    forced = (n_idx == 0) | (n_idx == cur) | (n_idx == cur - 1)
    admissible = (n_idx <= cur) & (n_idx < n_slc)

    for g in range(NSA_GROUPS):
        h0 = g * J
        q_raw = qraw_ref[0, h0:h0 + J].reshape(L, HEAD_DIM)
        s_c = lax.dot_general(kcmp_ref[0, g], q_raw, _NT, preferred_element_type=F32)
        s_c = jnp.where(mask_c, s_c, -jnp.inf)
        m_c = jnp.max(s_c, axis=0, keepdims=True)
        m_c = jnp.where(m_c > -jnp.inf, m_c, 0.0)
        p_c = jnp.exp(s_c - m_c)
        p_c = p_c / jnp.maximum(jnp.sum(p_c, axis=0, keepdims=True), 1e-30)
        o_c = lax.dot_general(vcmp_ref[0, g], p_c.astype(BF16), _TN, preferred_element_type=F32)

        p_sum = p_c[:, 0:TQ]
        for j in range(1, J):
            p_sum = p_sum + p_c[:, j * TQ:(j + 1) * TQ]
        imp = jnp.dot(ovt_ref[...], p_sum, preferred_element_type=F32,
                      precision=lax.Precision.HIGHEST)
        imp = jnp.where(forced, FORCE_SCORE, imp)
        imp = jnp.where(admissible, imp, -jnp.inf)
        rank = jnp.zeros((NSP, TQ), F32)
        for mrow in range(n_slc):
            other = imp[mrow:mrow + 1, :]
            ahead = (other > imp) | ((other == imp) & (n_idx > mrow))
            rank = rank + jnp.where(ahead, 1.0, 0.0)
        sel_s[g] = jnp.where(rank < n_sel, 1.0, 0.0)
        for j in range(J):
            oc_s[h0 + j] = o_c[:, j * TQ:(j + 1) * TQ]

    def rows(kc):
        return pl.ds(pl.multiple_of(kc * TK, TK), TK)

    tiles_per_chunk = TK // TQ
    far = WIN // TQ + 1
    rq = lax.broadcasted_iota(jnp.int32, (TQ, TQ), 0)
    cq = lax.broadcasted_iota(jnp.int32, (TQ, TQ), 1)
    for t in range(far + tiles_per_chunk):
        d = far - t
        ok = (rq <= cq + d * TQ) & (rq > cq - WIN + d * TQ)
        wb_s[t * TQ:(t + 1) * TQ, :] = jnp.where(ok, 0.0, NEG)

    def win_bias(kc):
        t0 = far - i + kc * tiles_per_chunk
        return wb_s[pl.ds(pl.multiple_of(t0 * TQ, TQ), TK), :]

    nk = ((i + 1) * TQ + TK - 1) // TK
    lo = (jnp.maximum(i - WIN // TQ, 0) * TQ) // TK
    per = TK // SLC_LEN
    for g in range(NSA_GROUPS):
        def bias_body(kc, _, g=g):
            picked = jnp.concatenate(
                [jnp.broadcast_to(sel_s[g, pl.ds(kc * per + r, 1), :], (SLC_LEN, TQ))
                 for r in range(per)], axis=0) > 0.5
            ok = picked & (kc * TK + row <= t_pos)
            bsel_s[g, rows(kc), :] = jnp.where(ok, 0.0, NEG)
            return 0

        lax.fori_loop(0, nk, bias_body, 0)

    def group_spec(g, k_ref_, vt_ref_, bias_chunk):
        return (g * J, J,
                lambda kc: k_ref_[0, g, rows(kc), :],
                lambda kc: vt_ref_[0, kc, g * HEAD_DIM:(g + 1) * HEAD_DIM, :],
                bias_chunk)

    _attend_groups(
        [group_spec(g, ks_ref, vst_ref, lambda kc, g=g: bsel_s[g, rows(kc), :])
         for g in range(NSA_GROUPS)],
        qrot_ref, 0, nk, m_s, l_s, os_s)
    _attend_groups(
        [group_spec(g, kw_ref, vwt_ref, win_bias) for g in range(NSA_GROUPS)],
        qrot_ref, lo, nk, m_s, l_s, ow_s)

    def gated(h):
        return (gt[3 * h:3 * h + 1, :] * oc_s[h] + gt[3 * h + 1:3 * h + 2, :] * os_s[h]
                + gt[3 * h + 2:3 * h + 3, :] * ow_s[h])

    for h in range(0, N_TOK_HEADS, 2):
        _write_pair_t(out_ref, h * HEAD_DIM, gated(h), gated(h + 1))
    _mem_attention_t(qm_ref, km_ref, vmt_ref, out_ref)


def _nsa_attn(qraw, qrot, kcmp, vcmp, ks, vst, kw, vwt, gt, qm, km, vmt, B, S):
    G = NSA_GROUPS
    NM = km.shape[2]
    NC = kcmp.shape[2]
    n_cmp = (S - CMP_LEN) // CMP_STRIDE + 1
    n_slc = S // SLC_LEN
    n_sel = min(SLC_TOP_MAX, n_slc)
    nsp = -(-n_slc // 8) * 8
    c0 = np.arange(NC) * CMP_STRIDE
    s0 = np.arange(nsp) * SLC_LEN
    ov = np.minimum(c0[None, :] + CMP_LEN, s0[:, None] + SLC_LEN) - np.maximum(c0[None, :], s0[:, None])
    ovt = (np.clip(ov, 0, None) / CMP_LEN).astype(np.float32)
    ovt[:, n_cmp:] = 0.0
    ovt[n_slc:, :] = 0.0

    qblk = lambda nh: pl.BlockSpec((1, nh, TQ, HEAD_DIM), lambda b, i: (b, 0, i, 0))
    full = pl.BlockSpec((1, G, S, HEAD_DIM), lambda b, i: (b, 0, 0, 0))
    full_t = pl.BlockSpec((1, S // TK, G * HEAD_DIM, TK), lambda b, i: (b, 0, 0, 0))
    cmpspec = pl.BlockSpec((1, G, NC, HEAD_DIM), lambda b, i: (b, 0, 0, 0))
    head_out = pltpu.VMEM((N_TOK_HEADS, HEAD_DIM, TQ), F32)
    stat = pltpu.VMEM((N_TOK_HEADS, 1, TQ), F32)
    return pl.pallas_call(
        functools.partial(_nsa_attn_kernel, n_cmp=n_cmp, n_slc=n_slc, n_sel=n_sel),
        grid=(B, S // TQ),
        in_specs=[
            qblk(N_TOK_HEADS), qblk(N_TOK_HEADS), cmpspec, cmpspec, full, full_t, full, full_t,
            pl.BlockSpec((1, GATE_ROWS, TQ), lambda b, i: (b, 0, i)),
            pl.BlockSpec((nsp, NC), lambda b, i: (0, 0)),
            qblk(N_MEM_HEADS),
            pl.BlockSpec((1, N_MEM_HEADS, NM, HEAD_DIM), lambda b, i: (b, 0, 0, 0)),
            pl.BlockSpec((1, MEM_Q, NM), lambda b, i: (b, 0, 0)),
        ],
        out_specs=pl.BlockSpec((1, TQ, TOK_W + MEM_Q), lambda b, i: (b, i, 0)),
        out_shape=jax.ShapeDtypeStruct((B, S, TOK_W + MEM_Q), BF16),
        scratch_shapes=[
            pltpu.VMEM((G, nsp, TQ), F32),
            pltpu.VMEM((G, S, TQ), F32),
            pltpu.VMEM(((WIN // TQ + 1 + TK // TQ) * TQ, TQ), F32),
            stat, stat,
            head_out, head_out, head_out,
        ],
        compiler_params=_cparams(2),
        name="nsa_attn",
    )(qraw, qrot, kcmp, vcmp, ks, vst, kw, vwt, gt, jnp.asarray(ovt), qm, km, vmt)


def _post_attn_kernel(x_ref, mix_ref, wo_ref, g_ref, win_ref, wdn_ref, gf_ref, out_ref, *, final):
    d_ff = wdn_ref.shape[0]
    x1 = x_ref[...] + jnp.dot(mix_ref[...], wo_ref[...], preferred_element_type=F32)
    h = _rms(x1, g_ref[...]).astype(BF16)
    gate = jnp.dot(h, win_ref[:, :d_ff], preferred_element_type=F32)
    up = jnp.dot(h, win_ref[:, d_ff:], preferred_element_type=F32)
    act = (jax.nn.silu(gate) * up).astype(BF16)
    x2 = x1 + jnp.dot(act, wdn_ref[...], preferred_element_type=F32)
    if final:
        x2 = _rms(x2, gf_ref[...])
    out_ref[...] = x2


def _post_attn(x2d, mix2d, w_o, g, w_in, w_down, g_final, final):
    N, D = x2d.shape
    MW = mix2d.shape[1]
    d_ff = w_down.shape[0]
    tm = min(TM_FFN, N)
    const = lambda shp: pl.BlockSpec(shp, lambda i: (0, 0), pipeline_mode=pl.Buffered(1))
    return pl.pallas_call(
        functools.partial(_post_attn_kernel, final=final),
        grid=(N // tm,),
        in_specs=[
            pl.BlockSpec((tm, D), lambda i: (i, 0)),
            pl.BlockSpec((tm, MW), lambda i: (i, 0)),
            const((MW, D)), const((1, D)), const((D, 2 * d_ff)), const((d_ff, D)), const((1, D)),
        ],
        out_specs=pl.BlockSpec((tm, D), lambda i: (i, 0)),
        out_shape=jax.ShapeDtypeStruct((N, D), F32),
        compiler_params=_cparams(1),
        name="post_attn",
    )(x2d, mix2d, w_o.astype(BF16), g.reshape(1, D), w_in.astype(BF16), w_down.astype(BF16),
      g_final.reshape(1, D))


def kernel(x, mem, attn_norm, mem_norm, ffn_norm, final_norm, dsa_w_in, dsa_ckv_norm, dsa_w_uk, dsa_w_uv, nsa_w_in, nsa_cmp_pos_k, nsa_cmp_pos_v, nsa_cmp_k_w1, nsa_cmp_k_w2, nsa_cmp_v_w1, nsa_cmp_v_w2, mem_w_kv, w_o, ffn_w_in, ffn_w_down):
    B, S, D = x.shape
    depth = attn_norm.shape[0]
    assert S % TM_PROJ == 0 or S < TM_PROJ
    assert S % TQ == 0 and (B * S) % TM_FFN == 0
    tabs = _rope_tables(S)
    x2d = x.reshape(B * S, D)
    for i in range(depth):
        km, vm = _mem_kv(mem, mem_norm[i], mem_w_kv[i])
        if i % 2 == 0:
            a = i // 2
            q, qi, kidx, k, v, wt, qm = _proj_dsa(
                x2d, attn_norm[i], dsa_w_in[a], dsa_ckv_norm[a], dsa_w_uk[a], dsa_w_uv[a], tabs, B, S)
            mix = _dsa_attn(q, qi, wt, kidx, k, v, qm, km, vm, B, S)
        else:
            b = i // 2
            qraw, qrot, kc, vc, ks, vs, kw, vw, gt, qm = _proj_nsa(
                x2d, attn_norm[i], nsa_w_in[b], tabs, B, S)
            kcmp, vcmp = _nsa_compress(
                kc, vc, nsa_cmp_pos_k[b], nsa_cmp_pos_v[b],
                nsa_cmp_k_w1[b], nsa_cmp_k_w2[b], nsa_cmp_v_w1[b], nsa_cmp_v_w2[b], B, S)
            mix = _nsa_attn(qraw, qrot, kcmp, vcmp, ks, vs, kw, vw, gt, qm, km, vm, B, S)
        x2d = _post_attn(x2d, mix.reshape(B * S, TOK_W + MEM_Q), w_o[i], ffn_norm[i],
                         ffn_w_in[i], ffn_w_down[i], final_norm, final=(i == depth - 1))
    return x2d.reshape(B, S, D)
```

```python
import functools
import math

import numpy as np
import jax
import jax.numpy as jnp
from jax import lax
from jax.experimental import pallas as pl
from jax.experimental.pallas import tpu as pltpu

F32 = jnp.float32
BF16 = jnp.bfloat16

HEAD_DIM = 64
ROPE_DIM = 16
ROPE_THETA = 500000.0
RMS_EPS = 1e-6
N_TOK_HEADS = 12
N_MEM_HEADS = 4
TOK_W = N_TOK_HEADS * HEAD_DIM
MEM_Q = N_MEM_HEADS * HEAD_DIM
DSA_KV_RANK = 128
DSA_NOPE = HEAD_DIM - ROPE_DIM
IDX_HEADS = 8
IDX_DIM = 64
DSA_TOPK_MAX = 256
NSA_GROUPS = 2
NSA_HPG = N_TOK_HEADS // NSA_GROUPS
CMP_LEN = 32
CMP_STRIDE = 16
CMP_HIDDEN = 128
SLC_LEN = 64
SLC_TOP_MAX = 16
WIN = 512
FORCE_SCORE = 1e9
N_GATES = N_TOK_HEADS * 3
GATE_ROWS = 40

LANE = 128
TQ = 256
TK = 256
ONES_ROWS = 16
VROWS = HEAD_DIM + ONES_ROWS
TM_PROJ = 512
TM_FFN = 256
VMEM_LIMIT = 56 * 1024 * 1024
NEG = -1e30
SCALE_LOG2 = HEAD_DIM ** -0.5 * math.log2(math.e)
INT_MIN = -2 ** 31

_NT = (((1,), (1,)), ((), ()))
_TN = (((0,), (0,)), ((), ()))


def _cparams(n_axes):
    return pltpu.CompilerParams(
        dimension_semantics=("arbitrary",) * n_axes, vmem_limit_bytes=VMEM_LIMIT)


def _rms(xf, g):
    ms = jnp.mean(xf * xf, axis=-1, keepdims=True)
    return xf * lax.rsqrt(ms + RMS_EPS) * g


def _rope_tables(S):
    inv = ROPE_THETA ** (-np.arange(0, ROPE_DIM, 2, dtype=np.float64) / ROPE_DIM)
    ang = np.arange(S, dtype=np.float64)[:, None] * inv[None, :]
    cos = np.cos(ang).astype(np.float32)
    sin = np.sin(ang).astype(np.float32)
    half = ROPE_DIM // 2
    c = np.ones((S, LANE), np.float32)
    s_up = np.zeros((S, LANE), np.float32)
    s_dn = np.zeros((S, LANE), np.float32)
    for base in range(0, LANE, HEAD_DIM):
        c[:, base:base + half] = cos
        c[:, base + half:base + ROPE_DIM] = cos
        s_up[:, base + half:base + ROPE_DIM] = sin
        s_dn[:, base:base + half] = -sin
    return jnp.asarray(c), jnp.asarray(s_up), jnp.asarray(s_dn)


def _rope(x, c, s_up, s_dn):
    half = ROPE_DIM // 2
    outs = []
    for j in range(x.shape[1] // LANE):
        xc = x[:, j * LANE:(j + 1) * LANE]
        outs.append(xc * c + pltpu.roll(xc, half, 1) * s_up + pltpu.roll(xc, LANE - half, 1) * s_dn)
    return outs


def _store_heads(ref, chunks, dtype):
    for j, ch in enumerate(chunks):
        ref[0, 2 * j, :, :] = ch[:, :HEAD_DIM].astype(dtype)
        ref[0, 2 * j + 1, :, :] = ch[:, HEAD_DIM:].astype(dtype)


def _chunks(x):
    return [x[:, j * LANE:(j + 1) * LANE] for j in range(x.shape[1] // LANE)]


def _store_values_t(ref, x, row_lo, n_groups):
    xt = x.T
    ones = jnp.ones((ONES_ROWS, TK), ref.dtype)
    for j in range(x.shape[0] // TK):
        for g in range(n_groups):
            r0 = row_lo + g * HEAD_DIM
            ref[0, j, g * VROWS:g * VROWS + HEAD_DIM, :] = (
                xt[r0:r0 + HEAD_DIM, j * TK:(j + 1) * TK].astype(ref.dtype))
            ref[0, j, g * VROWS + HEAD_DIM:(g + 1) * VROWS, :] = ones


def _proj_dsa_kernel(x_ref, g_ref, w_ref, gkv_ref, wkv_ref, c_ref, su_ref, sd_ref,
                     q_ref, qi_ref, kidx_ref, k_ref, vt_ref, wt_ref, qm_ref):
    h = _rms(x_ref[...], g_ref[...]).astype(BF16)
    c, su, sd = c_ref[...], su_ref[...], sd_ref[...]

    def mm(lo, hi):
        return jnp.dot(h, w_ref[:, lo:hi], preferred_element_type=F32)

    scale = HEAD_DIM ** -0.5
    q = _rope(mm(0, TOK_W), c, su, sd)
    _store_heads(q_ref, [ch * SCALE_LOG2 for ch in q], BF16)

    ckv = _rms(mm(TOK_W, TOK_W + DSA_KV_RANK), gkv_ref[...]).astype(BF16)
    kvn = jnp.dot(ckv, wkv_ref[...], preferred_element_type=F32)

    o_qi = TOK_W + DSA_KV_RANK
    qi = _rope(mm(o_qi, o_qi + IDX_HEADS * IDX_DIM), c, su, sd)
    _store_heads(qi_ref, qi, BF16)

    o_misc = o_qi + IDX_HEADS * IDX_DIM
    misc = _rope(mm(o_misc, o_misc + LANE), c, su, sd)[0]
    kidx_ref[0] = misc[:, :IDX_DIM].astype(BF16)
    lane = lax.broadcasted_iota(jnp.int32, misc.shape, 1)
    k_full = jnp.where(lane < ROPE_DIM, pltpu.roll(misc, HEAD_DIM, 1), kvn)
    k_ref[0] = k_full[:, :HEAD_DIM].astype(BF16)
    _store_values_t(vt_ref, kvn, HEAD_DIM, 1)
    w_scaled = misc * (IDX_HEADS ** -0.5 * IDX_DIM ** -0.5)
    w_row0 = IDX_DIM + ROPE_DIM
    wt_ref[0] = w_scaled.T[w_row0:w_row0 + IDX_HEADS, :]

    o_qm = o_misc + LANE
    qm = mm(o_qm, o_qm + MEM_Q)
    _store_heads(qm_ref, [ch * SCALE_LOG2 for ch in _chunks(qm)], BF16)


def _proj_dsa(x2d, g, w_in, gkv, w_uk, w_uv, tabs, B, S):
    D = x2d.shape[1]
    o = np.cumsum([0, TOK_W, DSA_KV_RANK, ROPE_DIM, IDX_HEADS * IDX_DIM, IDX_DIM, IDX_HEADS, MEM_Q])
    q, ckv, kr, qi, ki, wi, qm = [w_in[:, o[i]:o[i + 1]] for i in range(7)]
    pad = jnp.zeros((D, LANE - IDX_DIM - ROPE_DIM - IDX_HEADS), w_in.dtype)
    w = jnp.concatenate([q, ckv, qi, ki, kr, wi, pad, qm], axis=1).astype(BF16)
    wkv = jnp.concatenate([jnp.zeros((DSA_KV_RANK, ROPE_DIM), F32), w_uk, w_uv], axis=1).astype(BF16)
    ncol = w.shape[1]
    tm = min(TM_PROJ, S)
    nst = S // tm
    tok = lambda i: (i, 0)
    const = lambda i: (0, 0)
    tab = lambda i: (i % nst, 0)
    hm = lambda i: (i // nst, 0, i % nst, 0)
    row = lambda i: (i // nst, i % nst, 0)
    outs = pl.pallas_call(
        _proj_dsa_kernel,
        grid=(B * nst,),
        in_specs=[
            pl.BlockSpec((tm, D), tok),
            pl.BlockSpec((1, D), const),
            pl.BlockSpec((D, ncol), const),
            pl.BlockSpec((1, DSA_KV_RANK), const),
            pl.BlockSpec((DSA_KV_RANK, LANE), const),
            pl.BlockSpec((tm, LANE), tab),
            pl.BlockSpec((tm, LANE), tab),
            pl.BlockSpec((tm, LANE), tab),
        ],
        out_specs=[
            pl.BlockSpec((1, N_TOK_HEADS, tm, HEAD_DIM), hm),
            pl.BlockSpec((1, IDX_HEADS, tm, IDX_DIM), hm),
            pl.BlockSpec((1, tm, IDX_DIM), row),
            pl.BlockSpec((1, tm, HEAD_DIM), row),
            pl.BlockSpec((1, tm // TK, VROWS, TK), lambda i: (i // nst, i % nst, 0, 0)),
            pl.BlockSpec((1, IDX_HEADS, tm), lambda i: (i // nst, 0, i % nst)),
            pl.BlockSpec((1, N_MEM_HEADS, tm, HEAD_DIM), hm),
        ],
        out_shape=[
            jax.ShapeDtypeStruct((B, N_TOK_HEADS, S, HEAD_DIM), BF16),
            jax.ShapeDtypeStruct((B, IDX_HEADS, S, IDX_DIM), BF16),
            jax.ShapeDtypeStruct((B, S, IDX_DIM), BF16),
            jax.ShapeDtypeStruct((B, S, HEAD_DIM), BF16),
            jax.ShapeDtypeStruct((B, S // TK, VROWS, TK), BF16),
            jax.ShapeDtypeStruct((B, IDX_HEADS, S), F32),
            jax.ShapeDtypeStruct((B, N_MEM_HEADS, S, HEAD_DIM), BF16),
        ],
        compiler_params=_cparams(1),
        name="proj_dsa",
    )(x2d, g.reshape(1, D), w, gkv.reshape(1, DSA_KV_RANK), wkv, *tabs)
    return outs


def _proj_nsa_kernel(x_ref, g_ref, w_ref, c_ref, su_ref, sd_ref,
                     qraw_ref, qrot_ref, kc_ref, vc_ref, ks_ref, vs_ref, kw_ref, vw_ref,
                     gt_ref, qm_ref):
    h = _rms(x_ref[...], g_ref[...]).astype(BF16)
    c, su, sd = c_ref[...], su_ref[...], sd_ref[...]

    def mm(lo, hi):
        return jnp.dot(h, w_ref[:, lo:hi], preferred_element_type=F32)

    scale = HEAD_DIM ** -0.5
    q = mm(0, TOK_W)
    _store_heads(qraw_ref, [ch * scale for ch in _chunks(q)], BF16)
    _store_heads(qrot_ref, [ch * SCALE_LOG2 for ch in _rope(q, c, su, sd)], BF16)

    o = TOK_W
    _store_heads(kc_ref, _chunks(mm(o, o + LANE)), F32)
    _store_heads(vc_ref, _chunks(mm(o + LANE, o + 2 * LANE)), F32)
    _store_heads(ks_ref, _rope(mm(o + 2 * LANE, o + 3 * LANE), c, su, sd), BF16)
    _store_values_t(vs_ref, mm(o + 3 * LANE, o + 4 * LANE), 0, NSA_GROUPS)
    _store_heads(kw_ref, _rope(mm(o + 4 * LANE, o + 5 * LANE), c, su, sd), BF16)
    _store_values_t(vw_ref, mm(o + 5 * LANE, o + 6 * LANE), 0, NSA_GROUPS)

    gates = jax.nn.sigmoid(mm(o + 6 * LANE, o + 7 * LANE))
    gt_ref[0] = gates.T[:GATE_ROWS, :]

    qm = mm(o + 7 * LANE, o + 7 * LANE + MEM_Q)
    _store_heads(qm_ref, [ch * SCALE_LOG2 for ch in _chunks(qm)], BF16)


def _proj_nsa(x2d, g, w_in, tabs, B, S):
    D = x2d.shape[1]
    kv_w = NSA_GROUPS * HEAD_DIM
    o_g = TOK_W + 6 * kv_w
    pad = jnp.zeros((D, LANE - N_GATES), w_in.dtype)
    w = jnp.concatenate([w_in[:, :o_g + N_GATES], pad, w_in[:, o_g + N_GATES:]], axis=1).astype(BF16)
    ncol = w.shape[1]
    tm = min(TM_PROJ, S)
    nst = S // tm
    tok = lambda i: (i, 0)
    const = lambda i: (0, 0)
    tab = lambda i: (i % nst, 0)
    hm = lambda i: (i // nst, 0, i % nst, 0)

    def hm_spec(nh):
        return pl.BlockSpec((1, nh, tm, HEAD_DIM), hm)

    def hm_shape(nh, dt):
        return jax.ShapeDtypeStruct((B, nh, S, HEAD_DIM), dt)

    G = NSA_GROUPS
    vt_spec = pl.BlockSpec((1, tm // TK, G * VROWS, TK), lambda i: (i // nst, i % nst, 0, 0))
    vt_shape = jax.ShapeDtypeStruct((B, S // TK, G * VROWS, TK), BF16)
    outs = pl.pallas_call(
        _proj_nsa_kernel,
        grid=(B * nst,),
        in_specs=[
            pl.BlockSpec((tm, D), tok),
            pl.BlockSpec((1, D), const),
            pl.BlockSpec((D, ncol), const),
            pl.BlockSpec((tm, LANE), tab),
            pl.BlockSpec((tm, LANE), tab),
            pl.BlockSpec((tm, LANE), tab),
        ],
        out_specs=[
            hm_spec(N_TOK_HEADS), hm_spec(N_TOK_HEADS),
            hm_spec(G), hm_spec(G), hm_spec(G), vt_spec, hm_spec(G), vt_spec,
            pl.BlockSpec((1, GATE_ROWS, tm), lambda i: (i // nst, 0, i % nst)),
            hm_spec(N_MEM_HEADS),
        ],
        out_shape=[
            hm_shape(N_TOK_HEADS, BF16), hm_shape(N_TOK_HEADS, BF16),
            hm_shape(G, F32), hm_shape(G, F32),
            hm_shape(G, BF16), vt_shape, hm_shape(G, BF16), vt_shape,
            jax.ShapeDtypeStruct((B, GATE_ROWS, S), F32),
            hm_shape(N_MEM_HEADS, BF16),
        ],
        compiler_params=_cparams(1),
        name="proj_nsa",
    )(x2d, g.reshape(1, D), w, *tabs)
    return outs


def _mem_kv_kernel(m_ref, g_ref, w_ref, k_ref, vt_ref):
    h = _rms(m_ref[0], g_ref[...]).astype(BF16)
    kv = jnp.dot(h, w_ref[...], preferred_element_type=F32)
    _store_heads(k_ref, _chunks(kv[:, :MEM_Q]), BF16)
    ones = jnp.ones((ONES_ROWS, kv.shape[0]), BF16)
    for j in range(MEM_Q // LANE):
        pair_t = kv[:, MEM_Q + j * LANE:MEM_Q + (j + 1) * LANE].T.astype(BF16)
        for r in range(LANE // HEAD_DIM):
            h = j * (LANE // HEAD_DIM) + r
            vt_ref[0, h * VROWS:h * VROWS + HEAD_DIM, :] = pair_t[r * HEAD_DIM:(r + 1) * HEAD_DIM, :]
            vt_ref[0, h * VROWS + HEAD_DIM:(h + 1) * VROWS, :] = ones


def _mem_kv(mem, g, w_kv):
    B, NM, D = mem.shape
    assert NM % LANE == 0
    spec = pl.BlockSpec((1, N_MEM_HEADS, NM, HEAD_DIM), lambda b: (b, 0, 0, 0))
    shape = jax.ShapeDtypeStruct((B, N_MEM_HEADS, NM, HEAD_DIM), BF16)
    vt_spec = pl.BlockSpec((1, N_MEM_HEADS * VROWS, NM), lambda b: (b, 0, 0))
    vt_shape = jax.ShapeDtypeStruct((B, N_MEM_HEADS * VROWS, NM), BF16)
    return pl.pallas_call(
        _mem_kv_kernel,
        grid=(B,),
        in_specs=[
            pl.BlockSpec((1, NM, D), lambda b: (b, 0, 0)),
            pl.BlockSpec((1, D), lambda b: (0, 0)),
            pl.BlockSpec((D, 2 * MEM_Q), lambda b: (0, 0)),
        ],
        out_specs=[spec, vt_spec],
        out_shape=[shape, vt_shape],
        compiler_params=_cparams(1),
        name="mem_kv",
    )(mem, g.reshape(1, D), w_kv.astype(BF16))


def _write_pair_t(out_ref, col, o_a, o_b):
    pair = jnp.concatenate([o_a, o_b], axis=0)
    out_ref[0, :, col:col + 2 * HEAD_DIM] = pair.T.astype(out_ref.dtype)


def _mem_attention_t(qm_ref, km_ref, vmt_ref, out_ref):
    outs = []
    for h in range(N_MEM_HEADS):
        s = lax.dot_general(km_ref[0, h], qm_ref[0, h], _NT, preferred_element_type=F32)
        m = jnp.max(s, axis=0, keepdims=True)
        p = jnp.exp2(s - m)
        vt = vmt_ref[0, h * VROWS:(h + 1) * VROWS, :]
        outs.append(_normalised(jnp.dot(vt, p.astype(BF16), preferred_element_type=F32)))
    for h in range(0, N_MEM_HEADS, 2):
        _write_pair_t(out_ref, TOK_W + h * HEAD_DIM, outs[h], outs[h + 1])


def _normalised(acc):
    return acc[:HEAD_DIM] / acc[HEAD_DIM:HEAD_DIM + 1]


def _attend_groups(groups, q_ref, lo, hi, m_s, acc_s):
    heads = [h for (h0, nh, _, _, _) in groups for h in range(h0, h0 + nh)]
    for h in heads:
        m_s[h] = jnp.full((1, TQ), NEG, F32)
        acc_s[h] = jnp.zeros((VROWS, TQ), F32)

    def body(kc, _):
        scores = []
        for (h0, nh, k_chunk, _, bias_chunk) in groups:
            q_g = q_ref[0, h0:h0 + nh].reshape(nh * TQ, HEAD_DIM)
            scores.append(lax.dot_general(k_chunk(kc), q_g, _NT, preferred_element_type=F32))
        for (h0, nh, _, vt_chunk, bias_chunk), s_g in zip(groups, scores):
            bias = bias_chunk(kc)
            probs, alphas = [], []
            for j in range(nh):
                h = h0 + j
                s = s_g[:, j * TQ:(j + 1) * TQ] + bias
                m = m_s[h]
                m_new = jnp.maximum(m, jnp.max(s, axis=0, keepdims=True))
                alpha = jnp.exp2(m - m_new)
                p = jnp.exp2(s - m_new)
                m_s[h] = m_new
                probs.append(p.astype(BF16))
                alphas.append(alpha)
            pv = jnp.dot(vt_chunk(kc), jnp.concatenate(probs, axis=1), preferred_element_type=F32)
            for j in range(nh):
                acc_s[h0 + j] = alphas[j] * acc_s[h0 + j] + pv[:, j * TQ:(j + 1) * TQ]
        return 0

    lax.fori_loop(lo, hi, body, 0)
    for h in heads:
        acc_s[h, :HEAD_DIM, :] = _normalised(acc_s[h])


def _dsa_attn_kernel(q_ref, qi_ref, wt_ref, kidx_ref, k_ref, vt_ref, qm_ref, km_ref, vmt_ref,
                     out_ref, key_s, bias_s, m_s, o_s, *, topk):
    i = pl.program_id(1)
    nk = ((i + 1) * TQ + TK - 1) // TK
    t_pos = i * TQ + lax.broadcasted_iota(jnp.int32, (TK, TQ), 1)
    row = lax.broadcasted_iota(jnp.int32, (TK, TQ), 0)
    wt = wt_ref[0]

    def rows(kc):
        return pl.ds(pl.multiple_of(kc * TK, TK), TK)

    def score_body(kc, _):
        kx = kidx_ref[0, rows(kc), :]
        sc = jnp.zeros((TK, TQ), F32)
        for h in range(IDX_HEADS):
            lg = lax.dot_general(kx, qi_ref[0, h], _NT, preferred_element_type=F32)
            sc = sc + jnp.maximum(lg, 0.0) * wt[h:h + 1, :]
        sc = jnp.where(sc == 0.0, 0.0, sc)
        bits = pltpu.bitcast(sc, jnp.int32)
        key = bits ^ ((bits >> 31) & 0x7FFFFFFF)
        key_s[rows(kc), :] = jnp.where(kc * TK + row <= t_pos, key, INT_MIN)
        return 0

    lax.fori_loop(0, nk, score_body, 0)

    n_acc = 4 * 8

    def count(pred_fn):
        def body(kc, c):
            hit = jnp.where(pred_fn(key_s[rows(kc), :]), 1.0, 0.0)
            return c + jnp.sum(hit.reshape(TK // n_acc, n_acc, TQ), axis=0)

        c = lax.fori_loop(0, nk, body, jnp.zeros((n_acc, TQ), F32))
        return jnp.sum(c, axis=0, keepdims=True)

    def bit_body(b, carry):
        ans, n_ge = carry
        cand = ans + jnp.left_shift(jnp.int32(1), 31 - b)
        cnt = count(lambda keys: keys >= cand)
        take = cnt >= topk
        return jnp.where(take, cand, ans), jnp.where(take, cnt, n_ge)

    thr, n_ge = lax.fori_loop(
        0, 32, bit_body, (jnp.full((1, TQ), INT_MIN, jnp.int32), jnp.full((1, TQ), float(topk), F32)))
    has_ties = jnp.max(jnp.where((n_ge > topk) & (thr > INT_MIN), 1.0, 0.0)) > 0.0

    @pl.when(jnp.logical_not(has_ties))
    def _():
        def bias_body(kc, _):
            sel = (key_s[rows(kc), :] >= thr) & (kc * TK + row <= t_pos)
            bias_s[rows(kc), :] = jnp.where(sel, 0.0, NEG)
            return 0

        lax.fori_loop(0, nk, bias_body, 0)

    @pl.when(has_ties)
    def _():
        need = topk - count(lambda keys: keys > thr)
        tri = (lax.broadcasted_iota(jnp.int32, (TK, TK), 1)
               < lax.broadcasted_iota(jnp.int32, (TK, TK), 0)).astype(BF16)

        def bias_body(kc, eq_seen):
            keys = key_s[rows(kc), :]
            eq = keys == thr
            eq_f = jnp.where(eq, 1.0, 0.0)
            before = eq_seen + jnp.dot(tri, eq_f.astype(BF16), preferred_element_type=F32)
            sel = (keys > thr) | (eq & (before < need))
            sel = sel & (kc * TK + row <= t_pos)
            bias_s[rows(kc), :] = jnp.where(sel, 0.0, NEG)
            return eq_seen + jnp.sum(eq_f, axis=0, keepdims=True)

        lax.fori_loop(0, nk, bias_body, jnp.zeros((1, TQ), F32))

    _attend_groups(
        [(0, N_TOK_HEADS,
          lambda kc: k_ref[0, rows(kc), :],
          lambda kc: vt_ref[0, kc],
          lambda kc: bias_s[rows(kc), :])],
        q_ref, 0, nk, m_s, o_s)
    for h in range(0, N_TOK_HEADS, 2):
        _write_pair_t(out_ref, h * HEAD_DIM, o_s[h, :HEAD_DIM, :], o_s[h + 1, :HEAD_DIM, :])
    _mem_attention_t(qm_ref, km_ref, vmt_ref, out_ref)


def _dsa_attn(q, qi, wt, kidx, k, vt, qm, km, vmt, B, S):
    NM = km.shape[2]
    topk = min(DSA_TOPK_MAX, S // 4)
    assert S % TK == 0
    key_rows = S
    stat = pltpu.VMEM((N_TOK_HEADS, 1, TQ), F32)
    qblk = lambda nh: pl.BlockSpec((1, nh, TQ, HEAD_DIM), lambda b, i: (b, 0, i, 0))
    full = pl.BlockSpec((1, S, HEAD_DIM), lambda b, i: (b, 0, 0))
    return pl.pallas_call(
        functools.partial(_dsa_attn_kernel, topk=topk),
        grid=(B, S // TQ),
        in_specs=[
            qblk(N_TOK_HEADS), qblk(IDX_HEADS),
            pl.BlockSpec((1, IDX_HEADS, TQ), lambda b, i: (b, 0, i)),
            full, full,
            pl.BlockSpec((1, S // TK, VROWS, TK), lambda b, i: (b, 0, 0, 0)),
            qblk(N_MEM_HEADS),
            pl.BlockSpec((1, N_MEM_HEADS, NM, HEAD_DIM), lambda b, i: (b, 0, 0, 0)),
            pl.BlockSpec((1, N_MEM_HEADS * VROWS, NM), lambda b, i: (b, 0, 0)),
        ],
        out_specs=pl.BlockSpec((1, TQ, TOK_W + MEM_Q), lambda b, i: (b, i, 0)),
        out_shape=jax.ShapeDtypeStruct((B, S, TOK_W + MEM_Q), BF16),
        scratch_shapes=[
            pltpu.VMEM((key_rows, TQ), jnp.int32),
            pltpu.VMEM((S, TQ), F32),
            stat,
            pltpu.VMEM((N_TOK_HEADS, VROWS, TQ), F32),
        ],
        compiler_params=_cparams(2),
        name="dsa_attn",
    )(q, qi, wt, kidx, k, vt, qm, km, vmt)


def _nsa_compress_kernel(kc_ref, vc_ref, pk_ref, pv_ref, kw1_ref, kw2_ref, vw1_ref, vw2_ref,
                         ko_ref, vo_ref):
    half = (CMP_LEN // 2) * HEAD_DIM

    def run(x_ref, pos_ref, w1_ref, w2_ref, o_ref):
        r = x_ref[0, 0]
        nxt = pltpu.roll(r, r.shape[0] - 1, 0)
        a = (r + pos_ref[0:1, :]).astype(BF16)
        b = (nxt + pos_ref[1:2, :]).astype(BF16)
        hid = (jnp.dot(a, w1_ref[:half, :], preferred_element_type=F32)
               + jnp.dot(b, w1_ref[half:, :], preferred_element_type=F32))
        hid = jax.nn.gelu(hid).astype(BF16)
        o_ref[0, 0] = jnp.dot(hid, w2_ref[...], preferred_element_type=F32).astype(o_ref.dtype)

    run(kc_ref, pk_ref, kw1_ref, kw2_ref, ko_ref)
    run(vc_ref, pv_ref, vw1_ref, vw2_ref, vo_ref)


def _nsa_compress(kc, vc, pos_k, pos_v, k_w1, k_w2, v_w1, v_w2, B, S):
    G = NSA_GROUPS
    R = S // CMP_STRIDE
    W = CMP_STRIDE * HEAD_DIM
    kc = kc.reshape(B, G, R, W)
    vc = vc.reshape(B, G, R, W)
    xs = pl.BlockSpec((1, 1, R, W), lambda b, g: (b, g, 0, 0))
    cst = lambda shp: pl.BlockSpec(shp, lambda b, g: (0, 0))
    osz = pl.BlockSpec((1, 1, R, HEAD_DIM), lambda b, g: (b, g, 0, 0))
    osh = jax.ShapeDtypeStruct((B, G, R, HEAD_DIM), BF16)
    return pl.pallas_call(
        _nsa_compress_kernel,
        grid=(B, G),
        in_specs=[xs, xs, cst((2, W)), cst((2, W)),
                  cst((CMP_LEN * HEAD_DIM, CMP_HIDDEN)), cst((CMP_HIDDEN, HEAD_DIM)),
                  cst((CMP_LEN * HEAD_DIM, CMP_HIDDEN)), cst((CMP_HIDDEN, HEAD_DIM))],
        out_specs=[osz, osz],
        out_shape=[osh, osh],
        compiler_params=_cparams(2),
        name="nsa_compress",
    )(kc, vc, pos_k.reshape(2, W), pos_v.reshape(2, W),
      k_w1.astype(BF16), k_w2.astype(BF16), v_w1.astype(BF16), v_w2.astype(BF16))


def _nsa_attn_kernel(qraw_ref, qrot_ref, kcmp_ref, vcmp_ref, ks_ref, vst_ref, kw_ref, vwt_ref,
                     gt_ref, ovt_ref, qm_ref, km_ref, vmt_ref, out_ref,
                     sel_s, bsel_s, wb_s, m_s, oc_s, os_s, ow_s, *, n_cmp, n_slc, n_sel):
    i = pl.program_id(1)
    J = NSA_HPG
    L = J * TQ
    NC = kcmp_ref.shape[2]
    NSP = sel_s.shape[1]
    t_row = i * TQ + lax.broadcasted_iota(jnp.int32, (1, TQ), 1)
    t_pos = i * TQ + lax.broadcasted_iota(jnp.int32, (TK, TQ), 1)
    row = lax.broadcasted_iota(jnp.int32, (TK, TQ), 0)
    gt = gt_ref[0]

    c_idx = lax.broadcasted_iota(jnp.int32, (NC, TQ), 0)
    t_c = i * TQ + lax.broadcasted_iota(jnp.int32, (NC, TQ), 1)
    mask_c1 = (c_idx * CMP_STRIDE + CMP_LEN - 1 <= t_c) & (c_idx < n_cmp)
    mask_c = jnp.concatenate([mask_c1] * J, axis=1)

    n_idx = lax.broadcasted_iota(jnp.int32, (NSP, TQ), 0)
    cur = t_row // SLC_LEN
    forced = (n_idx == 0) | (n_idx == cur) | (n_idx == cur - 1)
    admissible = (n_idx <= cur) & (n_idx < n_slc)

    for g in range(NSA_GROUPS):
        h0 = g * J
        q_raw = qraw_ref[0, h0:h0 + J].reshape(L, HEAD_DIM)
        s_c = lax.dot_general(kcmp_ref[0, g], q_raw, _NT, preferred_element_type=F32)
        s_c = jnp.where(mask_c, s_c, -jnp.inf)
        m_c = jnp.max(s_c, axis=0, keepdims=True)
        m_c = jnp.where(m_c > -jnp.inf, m_c, 0.0)
        p_c = jnp.exp(s_c - m_c)
        p_c = p_c / jnp.maximum(jnp.sum(p_c, axis=0, keepdims=True), 1e-30)
        o_c = lax.dot_general(vcmp_ref[0, g], p_c.astype(BF16), _TN, preferred_element_type=F32)

        p_sum = p_c[:, 0:TQ]
        for j in range(1, J):
            p_sum = p_sum + p_c[:, j * TQ:(j + 1) * TQ]
        imp = jnp.dot(ovt_ref[...], p_sum, preferred_element_type=F32,
                      precision=lax.Precision.HIGHEST)
        imp = jnp.where(forced, FORCE_SCORE, imp)
        imp = jnp.where(admissible, imp, -jnp.inf)
        rank = jnp.zeros((NSP, TQ), F32)
        for mrow in range(n_slc):
            other = imp[mrow:mrow + 1, :]
            ahead = (other > imp) | ((other == imp) & (n_idx > mrow))
            rank = rank + jnp.where(ahead, 1.0, 0.0)
        sel_s[g] = jnp.where(rank < n_sel, 1.0, 0.0)
        for j in range(J):
            oc_s[h0 + j] = o_c[:, j * TQ:(j + 1) * TQ]

    def rows(kc):
        return pl.ds(pl.multiple_of(kc * TK, TK), TK)

    tiles_per_chunk = TK // TQ
    far = WIN // TQ + 1
    rq = lax.broadcasted_iota(jnp.int32, (TQ, TQ), 0)
    cq = lax.broadcasted_iota(jnp.int32, (TQ, TQ), 1)
    for t in range(far + tiles_per_chunk):
        d = far - t
        ok = (rq <= cq + d * TQ) & (rq > cq - WIN + d * TQ)
        wb_s[t * TQ:(t + 1) * TQ, :] = jnp.where(ok, 0.0, NEG)

    def win_bias(kc):
        t0 = far - i + kc * tiles_per_chunk
        return wb_s[pl.ds(pl.multiple_of(t0 * TQ, TQ), TK), :]

    nk = ((i + 1) * TQ + TK - 1) // TK
    lo = (jnp.maximum(i - WIN // TQ, 0) * TQ) // TK
    per = TK // SLC_LEN
    for g in range(NSA_GROUPS):
        def bias_body(kc, _, g=g):
            picked = jnp.concatenate(
                [jnp.broadcast_to(sel_s[g, pl.ds(kc * per + r, 1), :], (SLC_LEN, TQ))
                 for r in range(per)], axis=0) > 0.5
            ok = picked & (kc * TK + row <= t_pos)
            bsel_s[g, rows(kc), :] = jnp.where(ok, 0.0, NEG)
            return 0

        lax.fori_loop(0, nk, bias_body, 0)

    def group_spec(g, k_ref_, vt_ref_, bias_chunk):
        return (g * J, J,
                lambda kc: k_ref_[0, g, rows(kc), :],
                lambda kc: vt_ref_[0, kc, g * VROWS:(g + 1) * VROWS, :],
                bias_chunk)

    _attend_groups(
        [group_spec(g, ks_ref, vst_ref, lambda kc, g=g: bsel_s[g, rows(kc), :])
         for g in range(NSA_GROUPS)],
        qrot_ref, 0, nk, m_s, os_s)
    _attend_groups(
        [group_spec(g, kw_ref, vwt_ref, win_bias) for g in range(NSA_GROUPS)],
        qrot_ref, lo, nk, m_s, ow_s)

    def gated(h):
        return (gt[3 * h:3 * h + 1, :] * oc_s[h] + gt[3 * h + 1:3 * h + 2, :] * os_s[h, :HEAD_DIM, :]
                + gt[3 * h + 2:3 * h + 3, :] * ow_s[h, :HEAD_DIM, :])

    for h in range(0, N_TOK_HEADS, 2):
        _write_pair_t(out_ref, h * HEAD_DIM, gated(h), gated(h + 1))
    _mem_attention_t(qm_ref, km_ref, vmt_ref, out_ref)


def _nsa_attn(qraw, qrot, kcmp, vcmp, ks, vst, kw, vwt, gt, qm, km, vmt, B, S):
    G = NSA_GROUPS
    NM = km.shape[2]
    NC = kcmp.shape[2]
    n_cmp = (S - CMP_LEN) // CMP_STRIDE + 1
    n_slc = S // SLC_LEN
    n_sel = min(SLC_TOP_MAX, n_slc)
    nsp = -(-n_slc // 8) * 8
    c0 = np.arange(NC) * CMP_STRIDE
    s0 = np.arange(nsp) * SLC_LEN
    ov = np.minimum(c0[None, :] + CMP_LEN, s0[:, None] + SLC_LEN) - np.maximum(c0[None, :], s0[:, None])
    ovt = (np.clip(ov, 0, None) / CMP_LEN).astype(np.float32)
    ovt[:, n_cmp:] = 0.0
    ovt[n_slc:, :] = 0.0

    qblk = lambda nh: pl.BlockSpec((1, nh, TQ, HEAD_DIM), lambda b, i: (b, 0, i, 0))
    full = pl.BlockSpec((1, G, S, HEAD_DIM), lambda b, i: (b, 0, 0, 0))
    full_t = pl.BlockSpec((1, S // TK, G * VROWS, TK), lambda b, i: (b, 0, 0, 0))
    cmpspec = pl.BlockSpec((1, G, NC, HEAD_DIM), lambda b, i: (b, 0, 0, 0))
    head_out = pltpu.VMEM((N_TOK_HEADS, HEAD_DIM, TQ), F32)
    head_acc = pltpu.VMEM((N_TOK_HEADS, VROWS, TQ), F32)
    stat = pltpu.VMEM((N_TOK_HEADS, 1, TQ), F32)
    return pl.pallas_call(
        functools.partial(_nsa_attn_kernel, n_cmp=n_cmp, n_slc=n_slc, n_sel=n_sel),
        grid=(B, S // TQ),
        in_specs=[
            qblk(N_TOK_HEADS), qblk(N_TOK_HEADS), cmpspec, cmpspec, full, full_t, full, full_t,
            pl.BlockSpec((1, GATE_ROWS, TQ), lambda b, i: (b, 0, i)),
            pl.BlockSpec((nsp, NC), lambda b, i: (0, 0)),
            qblk(N_MEM_HEADS),
            pl.BlockSpec((1, N_MEM_HEADS, NM, HEAD_DIM), lambda b, i: (b, 0, 0, 0)),
            pl.BlockSpec((1, N_MEM_HEADS * VROWS, NM), lambda b, i: (b, 0, 0)),
        ],
        out_specs=pl.BlockSpec((1, TQ, TOK_W + MEM_Q), lambda b, i: (b, i, 0)),
        out_shape=jax.ShapeDtypeStruct((B, S, TOK_W + MEM_Q), BF16),
        scratch_shapes=[
            pltpu.VMEM((G, nsp, TQ), F32),
            pltpu.VMEM((G, S, TQ), F32),
            pltpu.VMEM(((WIN // TQ + 1 + TK // TQ) * TQ, TQ), F32),
            stat,
            head_out, head_acc, head_acc,
        ],
        compiler_params=_cparams(2),
        name="nsa_attn",
    )(qraw, qrot, kcmp, vcmp, ks, vst, kw, vwt, gt, jnp.asarray(ovt), qm, km, vmt)


def _post_attn_kernel(x_ref, mix_ref, wo_ref, g_ref, win_ref, wdn_ref, gf_ref, out_ref, *, final):
    d_ff = wdn_ref.shape[0]
    x1 = x_ref[...] + jnp.dot(mix_ref[...], wo_ref[...], preferred_element_type=F32)
    h = _rms(x1, g_ref[...]).astype(BF16)
    gate = jnp.dot(h, win_ref[:, :d_ff], preferred_element_type=F32)
    up = jnp.dot(h, win_ref[:, d_ff:], preferred_element_type=F32)
    act = (jax.nn.silu(gate) * up).astype(BF16)
    x2 = x1 + jnp.dot(act, wdn_ref[...], preferred_element_type=F32)
    if final:
        x2 = _rms(x2, gf_ref[...])
    out_ref[...] = x2


def _post_attn(x2d, mix2d, w_o, g, w_in, w_down, g_final, final):
    N, D = x2d.shape
    MW = mix2d.shape[1]
    d_ff = w_down.shape[0]
    tm = min(TM_FFN, N)
    const = lambda shp: pl.BlockSpec(shp, lambda i: (0, 0), pipeline_mode=pl.Buffered(1))
    return pl.pallas_call(
        functools.partial(_post_attn_kernel, final=final),
        grid=(N // tm,),
        in_specs=[
            pl.BlockSpec((tm, D), lambda i: (i, 0)),
            pl.BlockSpec((tm, MW), lambda i: (i, 0)),
            const((MW, D)), const((1, D)), const((D, 2 * d_ff)), const((d_ff, D)), const((1, D)),
        ],
        out_specs=pl.BlockSpec((tm, D), lambda i: (i, 0)),
        out_shape=jax.ShapeDtypeStruct((N, D), F32),
        compiler_params=_cparams(1),
        name="post_attn",
    )(x2d, mix2d, w_o.astype(BF16), g.reshape(1, D), w_in.astype(BF16), w_down.astype(BF16),
      g_final.reshape(1, D))


def kernel(x, mem, attn_norm, mem_norm, ffn_norm, final_norm, dsa_w_in, dsa_ckv_norm, dsa_w_uk, dsa_w_uv, nsa_w_in, nsa_cmp_pos_k, nsa_cmp_pos_v, nsa_cmp_k_w1, nsa_cmp_k_w2, nsa_cmp_v_w1, nsa_cmp_v_w2, mem_w_kv, w_o, ffn_w_in, ffn_w_down):
    B, S, D = x.shape
    depth = attn_norm.shape[0]
    assert S % TM_PROJ == 0 or S < TM_PROJ
    assert S % TQ == 0 and (B * S) % TM_FFN == 0
    tabs = _rope_tables(S)
    x2d = x.reshape(B * S, D)
    for i in range(depth):
        km, vm = _mem_kv(mem, mem_norm[i], mem_w_kv[i])
        if i % 2 == 0:
            a = i // 2
            q, qi, kidx, k, v, wt, qm = _proj_dsa(
                x2d, attn_norm[i], dsa_w_in[a], dsa_ckv_norm[a], dsa_w_uk[a], dsa_w_uv[a], tabs, B, S)
            mix = _dsa_attn(q, qi, wt, kidx, k, v, qm, km, vm, B, S)
        else:
            b = i // 2
            qraw, qrot, kc, vc, ks, vs, kw, vw, gt, qm = _proj_nsa(
                x2d, attn_norm[i], nsa_w_in[b], tabs, B, S)
            kcmp, vcmp = _nsa_compress(
                kc, vc, nsa_cmp_pos_k[b], nsa_cmp_pos_v[b],
                nsa_cmp_k_w1[b], nsa_cmp_k_w2[b], nsa_cmp_v_w1[b], nsa_cmp_v_w2[b], B, S)
            mix = _nsa_attn(qraw, qrot, kcmp, vcmp, ks, vs, kw, vw, gt, qm, km, vm, B, S)
        x2d = _post_attn(x2d, mix.reshape(B * S, TOK_W + MEM_Q), w_o[i], ffn_norm[i],
                         ffn_w_in[i], ffn_w_down[i], final_norm, final=(i == depth - 1))
    return x2d.reshape(B, S, D)
```

```python
import functools
import math

import numpy as np
import jax
import jax.numpy as jnp
from jax import lax
from jax.experimental import pallas as pl
from jax.experimental.pallas import tpu as pltpu

F32 = jnp.float32
BF16 = jnp.bfloat16

HEAD_DIM = 64
ROPE_DIM = 16
ROPE_THETA = 500000.0
RMS_EPS = 1e-6
N_TOK_HEADS = 12
N_MEM_HEADS = 4
TOK_W = N_TOK_HEADS * HEAD_DIM
MEM_Q = N_MEM_HEADS * HEAD_DIM
DSA_KV_RANK = 128
DSA_NOPE = HEAD_DIM - ROPE_DIM
IDX_HEADS = 8
IDX_DIM = 64
DSA_TOPK_MAX = 256
NSA_GROUPS = 2
NSA_HPG = N_TOK_HEADS // NSA_GROUPS
CMP_LEN = 32
CMP_STRIDE = 16
CMP_HIDDEN = 128
SLC_LEN = 64
SLC_TOP_MAX = 16
WIN = 512
FORCE_SCORE = 1e9
N_GATES = N_TOK_HEADS * 3
GATE_ROWS = 40

LANE = 128
TQ = 256
TK = 256
ONES_ROWS = 16
VROWS = HEAD_DIM + ONES_ROWS
TM_PROJ = 512
TM_FFN = 512
VMEM_LIMIT = 56 * 1024 * 1024
NEG = -1e30
SCALE_LOG2 = HEAD_DIM ** -0.5 * math.log2(math.e)
INT_MIN = -2 ** 31

_NT = (((1,), (1,)), ((), ()))
_TN = (((0,), (0,)), ((), ()))


def _cparams(n_axes):
    return pltpu.CompilerParams(
        dimension_semantics=("arbitrary",) * n_axes, vmem_limit_bytes=VMEM_LIMIT)


def _rms(xf, g):
    ms = jnp.mean(xf * xf, axis=-1, keepdims=True)
    return xf * lax.rsqrt(ms + RMS_EPS) * g


def _rope_tables(S):
    inv = ROPE_THETA ** (-np.arange(0, ROPE_DIM, 2, dtype=np.float64) / ROPE_DIM)
    ang = np.arange(S, dtype=np.float64)[:, None] * inv[None, :]
    cos = np.cos(ang).astype(np.float32)
    sin = np.sin(ang).astype(np.float32)
    half = ROPE_DIM // 2
    c = np.ones((S, LANE), np.float32)
    s_up = np.zeros((S, LANE), np.float32)
    s_dn = np.zeros((S, LANE), np.float32)
    for base in range(0, LANE, HEAD_DIM):
        c[:, base:base + half] = cos
        c[:, base + half:base + ROPE_DIM] = cos
        s_up[:, base + half:base + ROPE_DIM] = sin
        s_dn[:, base:base + half] = -sin
    return jnp.asarray(c), jnp.asarray(s_up), jnp.asarray(s_dn)


def _rope(x, c, s_up, s_dn):
    half = ROPE_DIM // 2
    outs = []
    for j in range(x.shape[1] // LANE):
        xc = x[:, j * LANE:(j + 1) * LANE]
        outs.append(xc * c + pltpu.roll(xc, half, 1) * s_up + pltpu.roll(xc, LANE - half, 1) * s_dn)
    return outs


def _store_heads(ref, chunks, dtype):
    for j, ch in enumerate(chunks):
        ref[0, 2 * j, :, :] = ch[:, :HEAD_DIM].astype(dtype)
        ref[0, 2 * j + 1, :, :] = ch[:, HEAD_DIM:].astype(dtype)


def _chunks(x):
    return [x[:, j * LANE:(j + 1) * LANE] for j in range(x.shape[1] // LANE)]


def _store_values_t(ref, x, row_lo, n_groups):
    xt = x.T
    ones = jnp.ones((ONES_ROWS, TK), ref.dtype)
    for j in range(x.shape[0] // TK):
        for g in range(n_groups):
            r0 = row_lo + g * HEAD_DIM
            ref[0, j, g * VROWS:g * VROWS + HEAD_DIM, :] = (
                xt[r0:r0 + HEAD_DIM, j * TK:(j + 1) * TK].astype(ref.dtype))
            ref[0, j, g * VROWS + HEAD_DIM:(g + 1) * VROWS, :] = ones


def _proj_dsa_kernel(x_ref, g_ref, w_ref, gkv_ref, wkv_ref, c_ref, su_ref, sd_ref,
                     q_ref, qi_ref, kidx_ref, k_ref, vt_ref, wt_ref, qm_ref):
    h = _rms(x_ref[...], g_ref[...]).astype(BF16)
    c, su, sd = c_ref[...], su_ref[...], sd_ref[...]

    def mm(lo, hi):
        return jnp.dot(h, w_ref[:, lo:hi], preferred_element_type=F32)

    scale = HEAD_DIM ** -0.5
    q = _rope(mm(0, TOK_W), c, su, sd)
    _store_heads(q_ref, [ch * SCALE_LOG2 for ch in q], BF16)

    ckv = _rms(mm(TOK_W, TOK_W + DSA_KV_RANK), gkv_ref[...]).astype(BF16)
    kvn = jnp.dot(ckv, wkv_ref[...], preferred_element_type=F32)

    o_qi = TOK_W + DSA_KV_RANK
    qi = _rope(mm(o_qi, o_qi + IDX_HEADS * IDX_DIM), c, su, sd)
    _store_heads(qi_ref, qi, BF16)

    o_misc = o_qi + IDX_HEADS * IDX_DIM
    misc = _rope(mm(o_misc, o_misc + LANE), c, su, sd)[0]
    kidx_ref[0] = misc[:, :IDX_DIM].astype(BF16)
    lane = lax.broadcasted_iota(jnp.int32, misc.shape, 1)
    k_full = jnp.where(lane < ROPE_DIM, pltpu.roll(misc, HEAD_DIM, 1), kvn)
    k_ref[0] = k_full[:, :HEAD_DIM].astype(BF16)
    _store_values_t(vt_ref, kvn, HEAD_DIM, 1)
    w_scaled = misc * (IDX_HEADS ** -0.5 * IDX_DIM ** -0.5)
    w_row0 = IDX_DIM + ROPE_DIM
    wt_ref[0] = w_scaled.T[w_row0:w_row0 + IDX_HEADS, :]

    o_qm = o_misc + LANE
    qm = mm(o_qm, o_qm + MEM_Q)
    _store_heads(qm_ref, [ch * SCALE_LOG2 for ch in _chunks(qm)], BF16)


def _proj_dsa(x2d, g, w_in, gkv, w_uk, w_uv, tabs, B, S):
    D = x2d.shape[1]
    o = np.cumsum([0, TOK_W, DSA_KV_RANK, ROPE_DIM, IDX_HEADS * IDX_DIM, IDX_DIM, IDX_HEADS, MEM_Q])
    q, ckv, kr, qi, ki, wi, qm = [w_in[:, o[i]:o[i + 1]] for i in range(7)]
    pad = jnp.zeros((D, LANE - IDX_DIM - ROPE_DIM - IDX_HEADS), w_in.dtype)
    w = jnp.concatenate([q, ckv, qi, ki, kr, wi, pad, qm], axis=1).astype(BF16)
    wkv = jnp.concatenate([jnp.zeros((DSA_KV_RANK, ROPE_DIM), F32), w_uk, w_uv], axis=1).astype(BF16)
    ncol = w.shape[1]
    tm = min(TM_PROJ, S)
    nst = S // tm
    tok = lambda i: (i, 0)
    const = lambda i: (0, 0)
    tab = lambda i: (i % nst, 0)
    hm = lambda i: (i // nst, 0, i % nst, 0)
    row = lambda i: (i // nst, i % nst, 0)
    outs = pl.pallas_call(
        _proj_dsa_kernel,
        grid=(B * nst,),
        in_specs=[
            pl.BlockSpec((tm, D), tok),
            pl.BlockSpec((1, D), const),
            pl.BlockSpec((D, ncol), const),
            pl.BlockSpec((1, DSA_KV_RANK), const),
            pl.BlockSpec((DSA_KV_RANK, LANE), const),
            pl.BlockSpec((tm, LANE), tab),
            pl.BlockSpec((tm, LANE), tab),
            pl.BlockSpec((tm, LANE), tab),
        ],
        out_specs=[
            pl.BlockSpec((1, N_TOK_HEADS, tm, HEAD_DIM), hm),
            pl.BlockSpec((1, IDX_HEADS, tm, IDX_DIM), hm),
            pl.BlockSpec((1, tm, IDX_DIM), row),
            pl.BlockSpec((1, tm, HEAD_DIM), row),
            pl.BlockSpec((1, tm // TK, VROWS, TK), lambda i: (i // nst, i % nst, 0, 0)),
            pl.BlockSpec((1, IDX_HEADS, tm), lambda i: (i // nst, 0, i % nst)),
            pl.BlockSpec((1, N_MEM_HEADS, tm, HEAD_DIM), hm),
        ],
        out_shape=[
            jax.ShapeDtypeStruct((B, N_TOK_HEADS, S, HEAD_DIM), BF16),
            jax.ShapeDtypeStruct((B, IDX_HEADS, S, IDX_DIM), BF16),
            jax.ShapeDtypeStruct((B, S, IDX_DIM), BF16),
            jax.ShapeDtypeStruct((B, S, HEAD_DIM), BF16),
            jax.ShapeDtypeStruct((B, S // TK, VROWS, TK), BF16),
            jax.ShapeDtypeStruct((B, IDX_HEADS, S), F32),
            jax.ShapeDtypeStruct((B, N_MEM_HEADS, S, HEAD_DIM), BF16),
        ],
        compiler_params=_cparams(1),
        name="proj_dsa",
    )(x2d, g.reshape(1, D), w, gkv.reshape(1, DSA_KV_RANK), wkv, *tabs)
    return outs


def _proj_nsa_kernel(x_ref, g_ref, w_ref, c_ref, su_ref, sd_ref,
                     qraw_ref, qrot_ref, kc_ref, vc_ref, ks_ref, vs_ref, kw_ref, vw_ref,
                     gt_ref, qm_ref):
    h = _rms(x_ref[...], g_ref[...]).astype(BF16)
    c, su, sd = c_ref[...], su_ref[...], sd_ref[...]

    def mm(lo, hi):
        return jnp.dot(h, w_ref[:, lo:hi], preferred_element_type=F32)

    scale = HEAD_DIM ** -0.5
    q = mm(0, TOK_W)
    _store_heads(qraw_ref, [ch * scale for ch in _chunks(q)], BF16)
    _store_heads(qrot_ref, [ch * SCALE_LOG2 for ch in _rope(q, c, su, sd)], BF16)

    o = TOK_W
    _store_heads(kc_ref, _chunks(mm(o, o + LANE)), F32)
    _store_heads(vc_ref, _chunks(mm(o + LANE, o + 2 * LANE)), F32)
    _store_heads(ks_ref, _rope(mm(o + 2 * LANE, o + 3 * LANE), c, su, sd), BF16)
    _store_values_t(vs_ref, mm(o + 3 * LANE, o + 4 * LANE), 0, NSA_GROUPS)
    _store_heads(kw_ref, _rope(mm(o + 4 * LANE, o + 5 * LANE), c, su, sd), BF16)
    _store_values_t(vw_ref, mm(o + 5 * LANE, o + 6 * LANE), 0, NSA_GROUPS)

    gates = jax.nn.sigmoid(mm(o + 6 * LANE, o + 7 * LANE))
    gt_ref[0] = gates.T[:GATE_ROWS, :]

    qm = mm(o + 7 * LANE, o + 7 * LANE + MEM_Q)
    _store_heads(qm_ref, [ch * SCALE_LOG2 for ch in _chunks(qm)], BF16)


def _proj_nsa(x2d, g, w_in, tabs, B, S):
    D = x2d.shape[1]
    kv_w = NSA_GROUPS * HEAD_DIM
    o_g = TOK_W + 6 * kv_w
    pad = jnp.zeros((D, LANE - N_GATES), w_in.dtype)
    w = jnp.concatenate([w_in[:, :o_g + N_GATES], pad, w_in[:, o_g + N_GATES:]], axis=1).astype(BF16)
    ncol = w.shape[1]
    tm = min(TM_PROJ, S)
    nst = S // tm
    tok = lambda i: (i, 0)
    const = lambda i: (0, 0)
    tab = lambda i: (i % nst, 0)
    hm = lambda i: (i // nst, 0, i % nst, 0)

    def hm_spec(nh):
        return pl.BlockSpec((1, nh, tm, HEAD_DIM), hm)

    def hm_shape(nh, dt):
        return jax.ShapeDtypeStruct((B, nh, S, HEAD_DIM), dt)

    G = NSA_GROUPS
    vt_spec = pl.BlockSpec((1, tm // TK, G * VROWS, TK), lambda i: (i // nst, i % nst, 0, 0))
    vt_shape = jax.ShapeDtypeStruct((B, S // TK, G * VROWS, TK), BF16)
    outs = pl.pallas_call(
        _proj_nsa_kernel,
        grid=(B * nst,),
        in_specs=[
            pl.BlockSpec((tm, D), tok),
            pl.BlockSpec((1, D), const),
            pl.BlockSpec((D, ncol), const),
            pl.BlockSpec((tm, LANE), tab),
            pl.BlockSpec((tm, LANE), tab),
            pl.BlockSpec((tm, LANE), tab),
        ],
        out_specs=[
            hm_spec(N_TOK_HEADS), hm_spec(N_TOK_HEADS),
            hm_spec(G), hm_spec(G), hm_spec(G), vt_spec, hm_spec(G), vt_spec,
            pl.BlockSpec((1, GATE_ROWS, tm), lambda i: (i // nst, 0, i % nst)),
            hm_spec(N_MEM_HEADS),
        ],
        out_shape=[
            hm_shape(N_TOK_HEADS, BF16), hm_shape(N_TOK_HEADS, BF16),
            hm_shape(G, F32), hm_shape(G, F32),
            hm_shape(G, BF16), vt_shape, hm_shape(G, BF16), vt_shape,
            jax.ShapeDtypeStruct((B, GATE_ROWS, S), F32),
            hm_shape(N_MEM_HEADS, BF16),
        ],
        compiler_params=_cparams(1),
        name="proj_nsa",
    )(x2d, g.reshape(1, D), w, *tabs)
    return outs


def _mem_kv_kernel(m_ref, g_ref, w_ref, k_ref, vt_ref):
    h = _rms(m_ref[0], g_ref[...]).astype(BF16)
    kv = jnp.dot(h, w_ref[...], preferred_element_type=F32)
    _store_heads(k_ref, _chunks(kv[:, :MEM_Q]), BF16)
    ones = jnp.ones((ONES_ROWS, kv.shape[0]), BF16)
    for j in range(MEM_Q // LANE):
        pair_t = kv[:, MEM_Q + j * LANE:MEM_Q + (j + 1) * LANE].T.astype(BF16)
        for r in range(LANE // HEAD_DIM):
            h = j * (LANE // HEAD_DIM) + r
            vt_ref[0, h * VROWS:h * VROWS + HEAD_DIM, :] = pair_t[r * HEAD_DIM:(r + 1) * HEAD_DIM, :]
            vt_ref[0, h * VROWS + HEAD_DIM:(h + 1) * VROWS, :] = ones


def _mem_kv(mem, g, w_kv):
    B, NM, D = mem.shape
    assert NM % LANE == 0
    spec = pl.BlockSpec((1, N_MEM_HEADS, NM, HEAD_DIM), lambda b: (b, 0, 0, 0))
    shape = jax.ShapeDtypeStruct((B, N_MEM_HEADS, NM, HEAD_DIM), BF16)
    vt_spec = pl.BlockSpec((1, N_MEM_HEADS * VROWS, NM), lambda b: (b, 0, 0))
    vt_shape = jax.ShapeDtypeStruct((B, N_MEM_HEADS * VROWS, NM), BF16)
    return pl.pallas_call(
        _mem_kv_kernel,
        grid=(B,),
        in_specs=[
            pl.BlockSpec((1, NM, D), lambda b: (b, 0, 0)),
            pl.BlockSpec((1, D), lambda b: (0, 0)),
            pl.BlockSpec((D, 2 * MEM_Q), lambda b: (0, 0)),
        ],
        out_specs=[spec, vt_spec],
        out_shape=[shape, vt_shape],
        compiler_params=_cparams(1),
        name="mem_kv",
    )(mem, g.reshape(1, D), w_kv.astype(BF16))


def _write_pair_t(out_ref, col, o_a, o_b):
    pair = jnp.concatenate([o_a, o_b], axis=0)
    out_ref[0, :, col:col + 2 * HEAD_DIM] = pair.T.astype(out_ref.dtype)


def _mem_attention_t(qm_ref, km_ref, vmt_ref, out_ref):
    outs = []
    for h in range(N_MEM_HEADS):
        s = lax.dot_general(km_ref[0, h], qm_ref[0, h], _NT, preferred_element_type=F32)
        m = jnp.max(s, axis=0, keepdims=True)
        p = jnp.exp2(s - m)
        vt = vmt_ref[0, h * VROWS:(h + 1) * VROWS, :]
        outs.append(_normalised(jnp.dot(vt, p.astype(BF16), preferred_element_type=F32)))
    for h in range(0, N_MEM_HEADS, 2):
        _write_pair_t(out_ref, TOK_W + h * HEAD_DIM, outs[h], outs[h + 1])


def _normalised(acc):
    return acc[:HEAD_DIM] / acc[HEAD_DIM:HEAD_DIM + 1]


def _attend_groups(groups, q_ref, lo, hi, s_s, m_s, acc_s):
    heads = [h for (h0, nh, _, _, _) in groups for h in range(h0, h0 + nh)]
    for h in heads:
        m_s[h] = jnp.full((1, TQ), NEG, F32)
        acc_s[h] = jnp.zeros((VROWS, TQ), F32)

    def body(kc, _):
        for (h0, nh, k_chunk, _, _) in groups:
            q_g = q_ref[0, h0:h0 + nh].reshape(nh * TQ, HEAD_DIM)
            s_s[:, h0 * TQ:(h0 + nh) * TQ] = lax.dot_general(
                k_chunk(kc), q_g, _NT, preferred_element_type=F32)
        for (h0, nh, _, vt_chunk, bias_chunk) in groups:
            for h in range(h0, h0 + nh):
                cols = slice(h * TQ, (h + 1) * TQ)
                m = m_s[h]
                m_new = jnp.maximum(m, jnp.max(s_s[:, cols] + bias_chunk(kc), axis=0, keepdims=True))
                m_s[h] = m_new
                p = jnp.exp2((s_s[:, cols] + bias_chunk(kc)) - m_new).astype(BF16)
                pv = jnp.dot(vt_chunk(kc), p, preferred_element_type=F32)
                acc_s[h] = jnp.exp2(m - m_new) * acc_s[h] + pv
        return 0

    lax.fori_loop(lo, hi, body, 0)
    for h in heads:
        acc_s[h, :HEAD_DIM, :] = _normalised(acc_s[h])


def _dsa_attn_kernel(q_ref, qi_ref, wt_ref, kidx_ref, k_ref, vt_ref, qm_ref, km_ref, vmt_ref,
                     out_ref, key_s, top_s, bias_s, s_s, m_s, o_s, *, topk):
    i = pl.program_id(1)
    nk = ((i + 1) * TQ + TK - 1) // TK
    t_pos = i * TQ + lax.broadcasted_iota(jnp.int32, (TK, TQ), 1)
    row = lax.broadcasted_iota(jnp.int32, (TK, TQ), 0)
    wt = wt_ref[0]

    def rows(kc):
        return pl.ds(pl.multiple_of(kc * TK, TK), TK)

    def score_body(kc, _):
        kx = kidx_ref[0, rows(kc), :]
        sc = jnp.zeros((TK, TQ), F32)
        for h in range(IDX_HEADS):
            lg = lax.dot_general(kx, qi_ref[0, h], _NT, preferred_element_type=F32)
            sc = sc + jnp.maximum(lg, 0.0) * wt[h:h + 1, :]
        sc = jnp.where(sc == 0.0, 0.0, sc)
        bits = pltpu.bitcast(sc, jnp.int32)
        key = bits ^ ((bits >> 31) & 0x7FFFFFFF)
        valid = kc * TK + row <= t_pos
        key_s[rows(kc), :] = jnp.where(valid, key, INT_MIN)
        top = pltpu.bitcast(bits & -65536, F32)
        top_s[rows(kc), :] = jnp.where(valid, top, jnp.nan).astype(BF16)
        return 0

    lax.fori_loop(0, nk, score_body, 0)

    n_acc = 4 * 8

    def count(pred_fn):
        def body(kc, c):
            hit = jnp.where(pred_fn(key_s[rows(kc), :]), 1.0, 0.0)
            return c + jnp.sum(hit.reshape(TK // n_acc, n_acc, TQ), axis=0)

        c = lax.fori_loop(0, nk, body, jnp.zeros((n_acc, TQ), F32))
        return jnp.sum(c, axis=0, keepdims=True)

    def count_top(cand):
        one, zero = jnp.ones((), BF16), jnp.zeros((), BF16)

        def body(kc, c):
            hit = jnp.where(top_s[rows(kc), :] >= cand, one, zero)
            parts = [hit[r * n_acc:(r + 1) * n_acc] for r in range(TK // n_acc)]
            while len(parts) > 1:
                parts = [a + b for a, b in zip(parts[0::2], parts[1::2])]
            return c + parts[0].astype(F32)

        c = lax.fori_loop(0, nk, body, jnp.zeros((n_acc, TQ), F32))
        return jnp.sum(c, axis=0, keepdims=True)

    def top_bit_body(b, carry):
        ans, n_ge = carry
        cand = ans + jnp.left_shift(jnp.int32(1), 15 - b)
        cand_bits = jnp.left_shift(cand ^ ((cand >> 31) & 0x7FFF), 16)
        cnt = count_top(pltpu.bitcast(cand_bits, F32).astype(BF16))
        take = cnt >= topk
        return jnp.where(take, cand, ans), jnp.where(take, cnt, n_ge)

    def bit_body(b, carry):
        ans, n_ge = carry
        cand = ans + jnp.left_shift(jnp.int32(1), 31 - b)
        cnt = count(lambda keys: keys >= cand)
        take = cnt >= topk
        return jnp.where(take, cand, ans), jnp.where(take, cnt, n_ge)

    top, n_ge = lax.fori_loop(
        0, 16, top_bit_body,
        (jnp.full((1, TQ), -2 ** 15, jnp.int32), jnp.full((1, TQ), float(topk), F32)))
    thr, n_ge = lax.fori_loop(16, 32, bit_body, (jnp.left_shift(top, 16), n_ge))
    has_ties = jnp.max(jnp.where((n_ge > topk) & (thr > INT_MIN), 1.0, 0.0)) > 0.0

    @pl.when(jnp.logical_not(has_ties))
    def _():
        def bias_body(kc, _):
            sel = (key_s[rows(kc), :] >= thr) & (kc * TK + row <= t_pos)
            bias_s[rows(kc), :] = jnp.where(sel, 0.0, NEG)
            return 0

        lax.fori_loop(0, nk, bias_body, 0)

    @pl.when(has_ties)
    def _():
        need = topk - count(lambda keys: keys > thr)
        tri = (lax.broadcasted_iota(jnp.int32, (TK, TK), 1)
               < lax.broadcasted_iota(jnp.int32, (TK, TK), 0)).astype(BF16)

        def bias_body(kc, eq_seen):
            keys = key_s[rows(kc), :]
            eq = keys == thr
            eq_f = jnp.where(eq, 1.0, 0.0)
            before = eq_seen + jnp.dot(tri, eq_f.astype(BF16), preferred_element_type=F32)
            sel = (keys > thr) | (eq & (before < need))
            sel = sel & (kc * TK + row <= t_pos)
            bias_s[rows(kc), :] = jnp.where(sel, 0.0, NEG)
            return eq_seen + jnp.sum(eq_f, axis=0, keepdims=True)

        lax.fori_loop(0, nk, bias_body, jnp.zeros((1, TQ), F32))

    _attend_groups(
        [(0, N_TOK_HEADS,
          lambda kc: k_ref[0, rows(kc), :],
          lambda kc: vt_ref[0, kc],
          lambda kc: bias_s[rows(kc), :])],
        q_ref, 0, nk, s_s, m_s, o_s)
    for h in range(0, N_TOK_HEADS, 2):
        _write_pair_t(out_ref, h * HEAD_DIM, o_s[h, :HEAD_DIM, :], o_s[h + 1, :HEAD_DIM, :])
    _mem_attention_t(qm_ref, km_ref, vmt_ref, out_ref)


def _dsa_attn(q, qi, wt, kidx, k, vt, qm, km, vmt, B, S):
    NM = km.shape[2]
    topk = min(DSA_TOPK_MAX, S // 4)
    assert S % TK == 0
    key_rows = S
    stat = pltpu.VMEM((N_TOK_HEADS, 1, TQ), F32)
    qblk = lambda nh: pl.BlockSpec((1, nh, TQ, HEAD_DIM), lambda b, i: (b, 0, i, 0))
    full = pl.BlockSpec((1, S, HEAD_DIM), lambda b, i: (b, 0, 0))
    return pl.pallas_call(
        functools.partial(_dsa_attn_kernel, topk=topk),
        grid=(B, S // TQ),
        in_specs=[
            qblk(N_TOK_HEADS), qblk(IDX_HEADS),
            pl.BlockSpec((1, IDX_HEADS, TQ), lambda b, i: (b, 0, i)),
            full, full,
            pl.BlockSpec((1, S // TK, VROWS, TK), lambda b, i: (b, 0, 0, 0)),
            qblk(N_MEM_HEADS),
            pl.BlockSpec((1, N_MEM_HEADS, NM, HEAD_DIM), lambda b, i: (b, 0, 0, 0)),
            pl.BlockSpec((1, N_MEM_HEADS * VROWS, NM), lambda b, i: (b, 0, 0)),
        ],
        out_specs=pl.BlockSpec((1, TQ, TOK_W + MEM_Q), lambda b, i: (b, i, 0)),
        out_shape=jax.ShapeDtypeStruct((B, S, TOK_W + MEM_Q), BF16),
        scratch_shapes=[
            pltpu.VMEM((key_rows, TQ), jnp.int32),
            pltpu.VMEM((key_rows, TQ), BF16),
            pltpu.VMEM((S, TQ), F32),
            pltpu.VMEM((TK, N_TOK_HEADS * TQ), F32),
            stat,
            pltpu.VMEM((N_TOK_HEADS, VROWS, TQ), F32),
        ],
        compiler_params=_cparams(2),
        name="dsa_attn",
    )(q, qi, wt, kidx, k, vt, qm, km, vmt)


def _nsa_compress_kernel(kc_ref, vc_ref, pk_ref, pv_ref, kw1_ref, kw2_ref, vw1_ref, vw2_ref,
                         ko_ref, vo_ref):
    half = (CMP_LEN // 2) * HEAD_DIM

    def run(x_ref, pos_ref, w1_ref, w2_ref, o_ref):
        r = x_ref[0, 0]
        nxt = pltpu.roll(r, r.shape[0] - 1, 0)
        a = (r + pos_ref[0:1, :]).astype(BF16)
        b = (nxt + pos_ref[1:2, :]).astype(BF16)
        hid = (jnp.dot(a, w1_ref[:half, :], preferred_element_type=F32)
               + jnp.dot(b, w1_ref[half:, :], preferred_element_type=F32))
        hid = jax.nn.gelu(hid).astype(BF16)
        o_ref[0, 0] = jnp.dot(hid, w2_ref[...], preferred_element_type=F32).astype(o_ref.dtype)

    run(kc_ref, pk_ref, kw1_ref, kw2_ref, ko_ref)
    run(vc_ref, pv_ref, vw1_ref, vw2_ref, vo_ref)


def _nsa_compress(kc, vc, pos_k, pos_v, k_w1, k_w2, v_w1, v_w2, B, S):
    G = NSA_GROUPS
    R = S // CMP_STRIDE
    W = CMP_STRIDE * HEAD_DIM
    kc = kc.reshape(B, G, R, W)
    vc = vc.reshape(B, G, R, W)
    xs = pl.BlockSpec((1, 1, R, W), lambda b, g: (b, g, 0, 0))
    cst = lambda shp: pl.BlockSpec(shp, lambda b, g: (0, 0))
    osz = pl.BlockSpec((1, 1, R, HEAD_DIM), lambda b, g: (b, g, 0, 0))
    osh = jax.ShapeDtypeStruct((B, G, R, HEAD_DIM), BF16)
    return pl.pallas_call(
        _nsa_compress_kernel,
        grid=(B, G),
        in_specs=[xs, xs, cst((2, W)), cst((2, W)),
                  cst((CMP_LEN * HEAD_DIM, CMP_HIDDEN)), cst((CMP_HIDDEN, HEAD_DIM)),
                  cst((CMP_LEN * HEAD_DIM, CMP_HIDDEN)), cst((CMP_HIDDEN, HEAD_DIM))],
        out_specs=[osz, osz],
        out_shape=[osh, osh],
        compiler_params=_cparams(2),
        name="nsa_compress",
    )(kc, vc, pos_k.reshape(2, W), pos_v.reshape(2, W),
      k_w1.astype(BF16), k_w2.astype(BF16), v_w1.astype(BF16), v_w2.astype(BF16))


def _nsa_attn_kernel(qraw_ref, qrot_ref, kcmp_ref, vcmp_ref, ks_ref, vst_ref, kw_ref, vwt_ref,
                     gt_ref, ovt_ref, qm_ref, km_ref, vmt_ref, out_ref,
                     sel_s, bsel_s, wb_s, s_s, m_s, oc_s, os_s, ow_s, *, n_cmp, n_slc, n_sel):
    i = pl.program_id(1)
    J = NSA_HPG
    L = J * TQ
    NC = kcmp_ref.shape[2]
    NSP = sel_s.shape[1]
    t_row = i * TQ + lax.broadcasted_iota(jnp.int32, (1, TQ), 1)
    t_pos = i * TQ + lax.broadcasted_iota(jnp.int32, (TK, TQ), 1)
    row = lax.broadcasted_iota(jnp.int32, (TK, TQ), 0)
    gt = gt_ref[0]

    c_idx = lax.broadcasted_iota(jnp.int32, (NC, TQ), 0)
    t_c = i * TQ + lax.broadcasted_iota(jnp.int32, (NC, TQ), 1)
    mask_c1 = (c_idx * CMP_STRIDE + CMP_LEN - 1 <= t_c) & (c_idx < n_cmp)
    mask_c = jnp.concatenate([mask_c1] * J, axis=1)

    n_idx = lax.broadcasted_iota(jnp.int32, (NSP, TQ), 0)
    cur = t_row // SLC_LEN
    forced = (n_idx == 0) | (n_idx == cur) | (n_idx == cur - 1)
    admissible = (n_idx <= cur) & (n_idx < n_slc)

    for g in range(NSA_GROUPS):
        h0 = g * J
        q_raw = qraw_ref[0, h0:h0 + J].reshape(L, HEAD_DIM)
        s_c = lax.dot_general(kcmp_ref[0, g], q_raw, _NT, preferred_element_type=F32)
        s_c = jnp.where(mask_c, s_c, -jnp.inf)
        m_c = jnp.max(s_c, axis=0, keepdims=True)
        m_c = jnp.where(m_c > -jnp.inf, m_c, 0.0)
        p_c = jnp.exp(s_c - m_c)
        p_c = p_c / jnp.maximum(jnp.sum(p_c, axis=0, keepdims=True), 1e-30)
        o_c = lax.dot_general(vcmp_ref[0, g], p_c.astype(BF16), _TN, preferred_element_type=F32)

        p_sum = p_c[:, 0:TQ]
        for j in range(1, J):
            p_sum = p_sum + p_c[:, j * TQ:(j + 1) * TQ]
        imp = jnp.dot(ovt_ref[...], p_sum, preferred_element_type=F32,
                      precision=lax.Precision.HIGHEST)
        imp = jnp.where(forced, FORCE_SCORE, imp)
        imp = jnp.where(admissible, imp, -jnp.inf)
        rank = jnp.zeros((NSP, TQ), F32)
        for mrow in range(n_slc):
            other = imp[mrow:mrow + 1, :]
            ahead = (other > imp) | ((other == imp) & (n_idx > mrow))
            rank = rank + jnp.where(ahead, 1.0, 0.0)
        sel_s[g] = jnp.where(rank < n_sel, 1.0, 0.0)
        for j in range(J):
            oc_s[h0 + j] = o_c[:, j * TQ:(j + 1) * TQ]

    def rows(kc):
        return pl.ds(pl.multiple_of(kc * TK, TK), TK)

    tiles_per_chunk = TK // TQ
    far = WIN // TQ + 1
    rq = lax.broadcasted_iota(jnp.int32, (TQ, TQ), 0)
    cq = lax.broadcasted_iota(jnp.int32, (TQ, TQ), 1)
    for t in range(far + tiles_per_chunk):
        d = far - t
        ok = (rq <= cq + d * TQ) & (rq > cq - WIN + d * TQ)
        wb_s[t * TQ:(t + 1) * TQ, :] = jnp.where(ok, 0.0, NEG)

    def win_bias(kc):
        t0 = far - i + kc * tiles_per_chunk
        return wb_s[pl.ds(pl.multiple_of(t0 * TQ, TQ), TK), :]

    nk = ((i + 1) * TQ + TK - 1) // TK
    lo = (jnp.maximum(i - WIN // TQ, 0) * TQ) // TK
    per = TK // SLC_LEN
    for g in range(NSA_GROUPS):
        def bias_body(kc, _, g=g):
            picked = jnp.concatenate(
                [jnp.broadcast_to(sel_s[g, pl.ds(kc * per + r, 1), :], (SLC_LEN, TQ))
                 for r in range(per)], axis=0) > 0.5
            ok = picked & (kc * TK + row <= t_pos)
            bsel_s[g, rows(kc), :] = jnp.where(ok, 0.0, NEG)
            return 0

        lax.fori_loop(0, nk, bias_body, 0)

    def group_spec(g, k_ref_, vt_ref_, bias_chunk):
        return (g * J, J,
                lambda kc: k_ref_[0, g, rows(kc), :],
                lambda kc: vt_ref_[0, kc, g * VROWS:(g + 1) * VROWS, :],
                bias_chunk)

    _attend_groups(
        [group_spec(g, ks_ref, vst_ref, lambda kc, g=g: bsel_s[g, rows(kc), :])
         for g in range(NSA_GROUPS)],
        qrot_ref, 0, nk, s_s, m_s, os_s)
    _attend_groups(
        [group_spec(g, kw_ref, vwt_ref, win_bias) for g in range(NSA_GROUPS)],
        qrot_ref, lo, nk, s_s, m_s, ow_s)

    def gated(h):
        return (gt[3 * h:3 * h + 1, :] * oc_s[h] + gt[3 * h + 1:3 * h + 2, :] * os_s[h, :HEAD_DIM, :]
                + gt[3 * h + 2:3 * h + 3, :] * ow_s[h, :HEAD_DIM, :])

    for h in range(0, N_TOK_HEADS, 2):
        _write_pair_t(out_ref, h * HEAD_DIM, gated(h), gated(h + 1))
    _mem_attention_t(qm_ref, km_ref, vmt_ref, out_ref)


def _nsa_attn(qraw, qrot, kcmp, vcmp, ks, vst, kw, vwt, gt, qm, km, vmt, B, S):
    G = NSA_GROUPS
    NM = km.shape[2]
    NC = kcmp.shape[2]
    n_cmp = (S - CMP_LEN) // CMP_STRIDE + 1
    n_slc = S // SLC_LEN
    n_sel = min(SLC_TOP_MAX, n_slc)
    nsp = -(-n_slc // 8) * 8
    c0 = np.arange(NC) * CMP_STRIDE
    s0 = np.arange(nsp) * SLC_LEN
    ov = np.minimum(c0[None, :] + CMP_LEN, s0[:, None] + SLC_LEN) - np.maximum(c0[None, :], s0[:, None])
    ovt = (np.clip(ov, 0, None) / CMP_LEN).astype(np.float32)
    ovt[:, n_cmp:] = 0.0
    ovt[n_slc:, :] = 0.0

    qblk = lambda nh: pl.BlockSpec((1, nh, TQ, HEAD_DIM), lambda b, i: (b, 0, i, 0))
    full = pl.BlockSpec((1, G, S, HEAD_DIM), lambda b, i: (b, 0, 0, 0))
    full_t = pl.BlockSpec((1, S // TK, G * VROWS, TK), lambda b, i: (b, 0, 0, 0))
    cmpspec = pl.BlockSpec((1, G, NC, HEAD_DIM), lambda b, i: (b, 0, 0, 0))
    head_out = pltpu.VMEM((N_TOK_HEADS, HEAD_DIM, TQ), F32)
    head_acc = pltpu.VMEM((N_TOK_HEADS, VROWS, TQ), F32)
    stat = pltpu.VMEM((N_TOK_HEADS, 1, TQ), F32)
    return pl.pallas_call(
        functools.partial(_nsa_attn_kernel, n_cmp=n_cmp, n_slc=n_slc, n_sel=n_sel),
        grid=(B, S // TQ),
        in_specs=[
            qblk(N_TOK_HEADS), qblk(N_TOK_HEADS), cmpspec, cmpspec, full, full_t, full, full_t,
            pl.BlockSpec((1, GATE_ROWS, TQ), lambda b, i: (b, 0, i)),
            pl.BlockSpec((nsp, NC), lambda b, i: (0, 0)),
            qblk(N_MEM_HEADS),
            pl.BlockSpec((1, N_MEM_HEADS, NM, HEAD_DIM), lambda b, i: (b, 0, 0, 0)),
            pl.BlockSpec((1, N_MEM_HEADS * VROWS, NM), lambda b, i: (b, 0, 0)),
        ],
        out_specs=pl.BlockSpec((1, TQ, TOK_W + MEM_Q), lambda b, i: (b, i, 0)),
        out_shape=jax.ShapeDtypeStruct((B, S, TOK_W + MEM_Q), BF16),
        scratch_shapes=[
            pltpu.VMEM((G, nsp, TQ), F32),
            pltpu.VMEM((G, S, TQ), F32),
            pltpu.VMEM(((WIN // TQ + 1 + TK // TQ) * TQ, TQ), F32),
            pltpu.VMEM((TK, N_TOK_HEADS * TQ), F32),
            stat,
            head_out, head_acc, head_acc,
        ],
        compiler_params=_cparams(2),
        name="nsa_attn",
    )(qraw, qrot, kcmp, vcmp, ks, vst, kw, vwt, gt, jnp.asarray(ovt), qm, km, vmt)


def _post_attn_kernel(x_ref, mix_ref, wo_ref, g_ref, win_ref, wdn_ref, gf_ref, out_ref, *, final):
    d_ff = wdn_ref.shape[0]
    x1 = x_ref[...] + jnp.dot(mix_ref[...], wo_ref[...], preferred_element_type=F32)
    h = _rms(x1, g_ref[...]).astype(BF16)
    gate = jnp.dot(h, win_ref[:, :d_ff], preferred_element_type=F32)
    up = jnp.dot(h, win_ref[:, d_ff:], preferred_element_type=F32)
    act = (jax.nn.silu(gate) * up).astype(BF16)
    x2 = x1 + jnp.dot(act, wdn_ref[...], preferred_element_type=F32)
    if final:
        x2 = _rms(x2, gf_ref[...])
    out_ref[...] = x2


def _post_attn(x2d, mix2d, w_o, g, w_in, w_down, g_final, final):
    N, D = x2d.shape
    MW = mix2d.shape[1]
    d_ff = w_down.shape[0]
    tm = min(TM_FFN, N)
    const = lambda shp: pl.BlockSpec(shp, lambda i: (0, 0), pipeline_mode=pl.Buffered(1))
    return pl.pallas_call(
        functools.partial(_post_attn_kernel, final=final),
        grid=(N // tm,),
        in_specs=[
            pl.BlockSpec((tm, D), lambda i: (i, 0)),
            pl.BlockSpec((tm, MW), lambda i: (i, 0)),
            const((MW, D)), const((1, D)), const((D, 2 * d_ff)), const((d_ff, D)), const((1, D)),
        ],
        out_specs=pl.BlockSpec((tm, D), lambda i: (i, 0)),
        out_shape=jax.ShapeDtypeStruct((N, D), F32),
        compiler_params=_cparams(1),
        name="post_attn",
    )(x2d, mix2d, w_o.astype(BF16), g.reshape(1, D), w_in.astype(BF16), w_down.astype(BF16),
      g_final.reshape(1, D))


def kernel(x, mem, attn_norm, mem_norm, ffn_norm, final_norm, dsa_w_in, dsa_ckv_norm, dsa_w_uk, dsa_w_uv, nsa_w_in, nsa_cmp_pos_k, nsa_cmp_pos_v, nsa_cmp_k_w1, nsa_cmp_k_w2, nsa_cmp_v_w1, nsa_cmp_v_w2, mem_w_kv, w_o, ffn_w_in, ffn_w_down):
    B, S, D = x.shape
    depth = attn_norm.shape[0]
    assert S % TM_PROJ == 0 or S < TM_PROJ
    assert S % TQ == 0 and (B * S) % TM_FFN == 0
    tabs = _rope_tables(S)
    x2d = x.reshape(B * S, D)
    for i in range(depth):
        km, vm = _mem_kv(mem, mem_norm[i], mem_w_kv[i])
        if i % 2 == 0:
            a = i // 2
            q, qi, kidx, k, v, wt, qm = _proj_dsa(
                x2d, attn_norm[i], dsa_w_in[a], dsa_ckv_norm[a], dsa_w_uk[a], dsa_w_uv[a], tabs, B, S)
            mix = _dsa_attn(q, qi, wt, kidx, k, v, qm, km, vm, B, S)
        else:
            b = i // 2
            qraw, qrot, kc, vc, ks, vs, kw, vw, gt, qm = _proj_nsa(
                x2d, attn_norm[i], nsa_w_in[b], tabs, B, S)
            kcmp, vcmp = _nsa_compress(
                kc, vc, nsa_cmp_pos_k[b], nsa_cmp_pos_v[b],
                nsa_cmp_k_w1[b], nsa_cmp_k_w2[b], nsa_cmp_v_w1[b], nsa_cmp_v_w2[b], B, S)
            mix = _nsa_attn(qraw, qrot, kcmp, vcmp, ks, vs, kw, vw, gt, qm, km, vm, B, S)
        x2d = _post_attn(x2d, mix.reshape(B * S, TOK_W + MEM_Q), w_o[i], ffn_norm[i],
                         ffn_w_in[i], ffn_w_down[i], final_norm, final=(i == depth - 1))
    return x2d.reshape(B, S, D)
```

```python
import functools
import math

import numpy as np
import jax
import jax.numpy as jnp
from jax import lax
from jax.experimental import pallas as pl
from jax.experimental.pallas import tpu as pltpu

F32 = jnp.float32
BF16 = jnp.bfloat16

HEAD_DIM = 64
ROPE_DIM = 16
ROPE_THETA = 500000.0
RMS_EPS = 1e-6
N_TOK_HEADS = 12
N_MEM_HEADS = 4
TOK_W = N_TOK_HEADS * HEAD_DIM
MEM_Q = N_MEM_HEADS * HEAD_DIM
DSA_KV_RANK = 128
DSA_NOPE = HEAD_DIM - ROPE_DIM
IDX_HEADS = 8
IDX_DIM = 64
DSA_TOPK_MAX = 256
NSA_GROUPS = 2
NSA_HPG = N_TOK_HEADS // NSA_GROUPS
CMP_LEN = 32
CMP_STRIDE = 16
CMP_HIDDEN = 128
SLC_LEN = 64
SLC_TOP_MAX = 16
WIN = 512
FORCE_SCORE = 1e9
N_GATES = N_TOK_HEADS * 3
GATE_ROWS = 40

LANE = 128
TQ = 256
TK = 256
ONES_ROWS = 16
VROWS = HEAD_DIM + ONES_ROWS
TM_PROJ = 512
TM_FFN = 512
VMEM_LIMIT = 56 * 1024 * 1024
NEG = -1e30
SCALE_LOG2 = HEAD_DIM ** -0.5 * math.log2(math.e)
INT_MIN = -2 ** 31

_NT = (((1,), (1,)), ((), ()))
_TN = (((0,), (0,)), ((), ()))


def _cparams(n_axes):
    return pltpu.CompilerParams(
        dimension_semantics=("arbitrary",) * n_axes, vmem_limit_bytes=VMEM_LIMIT)


def _rms(xf, g):
    ms = jnp.mean(xf * xf, axis=-1, keepdims=True)
    return xf * lax.rsqrt(ms + RMS_EPS) * g


def _rope_tables(S):
    inv = ROPE_THETA ** (-np.arange(0, ROPE_DIM, 2, dtype=np.float64) / ROPE_DIM)
    ang = np.arange(S, dtype=np.float64)[:, None] * inv[None, :]
    cos = np.cos(ang).astype(np.float32)
    sin = np.sin(ang).astype(np.float32)
    half = ROPE_DIM // 2
    c = np.ones((S, LANE), np.float32)
    s_up = np.zeros((S, LANE), np.float32)
    s_dn = np.zeros((S, LANE), np.float32)
    for base in range(0, LANE, HEAD_DIM):
        c[:, base:base + half] = cos
        c[:, base + half:base + ROPE_DIM] = cos
        s_up[:, base + half:base + ROPE_DIM] = sin
        s_dn[:, base:base + half] = -sin
    return jnp.asarray(c), jnp.asarray(s_up), jnp.asarray(s_dn)


def _rope(x, c, s_up, s_dn):
    half = ROPE_DIM // 2
    outs = []
    for j in range(x.shape[1] // LANE):
        xc = x[:, j * LANE:(j + 1) * LANE]
        outs.append(xc * c + pltpu.roll(xc, half, 1) * s_up + pltpu.roll(xc, LANE - half, 1) * s_dn)
    return outs


def _store_heads(ref, chunks, dtype):
    for j, ch in enumerate(chunks):
        ref[0, 2 * j, :, :] = ch[:, :HEAD_DIM].astype(dtype)
        ref[0, 2 * j + 1, :, :] = ch[:, HEAD_DIM:].astype(dtype)


def _chunks(x):
    return [x[:, j * LANE:(j + 1) * LANE] for j in range(x.shape[1] // LANE)]


def _store_values_t(ref, x, row_lo, n_groups):
    xt = x.T
    ones = jnp.ones((ONES_ROWS, TK), ref.dtype)
    for j in range(x.shape[0] // TK):
        for g in range(n_groups):
            r0 = row_lo + g * HEAD_DIM
            ref[0, j, g * VROWS:g * VROWS + HEAD_DIM, :] = (
                xt[r0:r0 + HEAD_DIM, j * TK:(j + 1) * TK].astype(ref.dtype))
            ref[0, j, g * VROWS + HEAD_DIM:(g + 1) * VROWS, :] = ones


def _proj_dsa_kernel(x_ref, g_ref, w_ref, gkv_ref, wkv_ref, c_ref, su_ref, sd_ref,
                     q_ref, qi_ref, kidx_ref, k_ref, vt_ref, wt_ref, qm_ref):
    h = _rms(x_ref[...], g_ref[...]).astype(BF16)
    c, su, sd = c_ref[...], su_ref[...], sd_ref[...]

    def mm(lo, hi):
        return jnp.dot(h, w_ref[:, lo:hi], preferred_element_type=F32)

    scale = HEAD_DIM ** -0.5
    q = _rope(mm(0, TOK_W), c, su, sd)
    _store_heads(q_ref, [ch * SCALE_LOG2 for ch in q], BF16)

    ckv = _rms(mm(TOK_W, TOK_W + DSA_KV_RANK), gkv_ref[...]).astype(BF16)
    kvn = jnp.dot(ckv, wkv_ref[...], preferred_element_type=F32)

    o_qi = TOK_W + DSA_KV_RANK
    qi = _rope(mm(o_qi, o_qi + IDX_HEADS * IDX_DIM), c, su, sd)
    _store_heads(qi_ref, qi, BF16)

    o_misc = o_qi + IDX_HEADS * IDX_DIM
    misc = _rope(mm(o_misc, o_misc + LANE), c, su, sd)[0]
    kidx_ref[0] = misc[:, :IDX_DIM].astype(BF16)
    lane = lax.broadcasted_iota(jnp.int32, misc.shape, 1)
    k_full = jnp.where(lane < ROPE_DIM, pltpu.roll(misc, HEAD_DIM, 1), kvn)
    k_ref[0] = k_full[:, :HEAD_DIM].astype(BF16)
    _store_values_t(vt_ref, kvn, HEAD_DIM, 1)
    w_scaled = misc * (IDX_HEADS ** -0.5 * IDX_DIM ** -0.5)
    w_row0 = IDX_DIM + ROPE_DIM
    wt_ref[0] = w_scaled.T[w_row0:w_row0 + IDX_HEADS, :]

    o_qm = o_misc + LANE
    qm = mm(o_qm, o_qm + MEM_Q)
    _store_heads(qm_ref, [ch * SCALE_LOG2 for ch in _chunks(qm)], BF16)


def _proj_dsa(x2d, g, w_in, gkv, w_uk, w_uv, tabs, B, S):
    D = x2d.shape[1]
    o = np.cumsum([0, TOK_W, DSA_KV_RANK, ROPE_DIM, IDX_HEADS * IDX_DIM, IDX_DIM, IDX_HEADS, MEM_Q])
    q, ckv, kr, qi, ki, wi, qm = [w_in[:, o[i]:o[i + 1]] for i in range(7)]
    pad = jnp.zeros((D, LANE - IDX_DIM - ROPE_DIM - IDX_HEADS), w_in.dtype)
    w = jnp.concatenate([q, ckv, qi, ki, kr, wi, pad, qm], axis=1).astype(BF16)
    wkv = jnp.concatenate([jnp.zeros((DSA_KV_RANK, ROPE_DIM), F32), w_uk, w_uv], axis=1).astype(BF16)
    ncol = w.shape[1]
    tm = min(TM_PROJ, S)
    nst = S // tm
    tok = lambda i: (i, 0)
    const = lambda i: (0, 0)
    tab = lambda i: (i % nst, 0)
    hm = lambda i: (i // nst, 0, i % nst, 0)
    row = lambda i: (i // nst, i % nst, 0)
    outs = pl.pallas_call(
        _proj_dsa_kernel,
        grid=(B * nst,),
        in_specs=[
            pl.BlockSpec((tm, D), tok),
            pl.BlockSpec((1, D), const),
            pl.BlockSpec((D, ncol), const),
            pl.BlockSpec((1, DSA_KV_RANK), const),
            pl.BlockSpec((DSA_KV_RANK, LANE), const),
            pl.BlockSpec((tm, LANE), tab),
            pl.BlockSpec((tm, LANE), tab),
            pl.BlockSpec((tm, LANE), tab),
        ],
        out_specs=[
            pl.BlockSpec((1, N_TOK_HEADS, tm, HEAD_DIM), hm),
            pl.BlockSpec((1, IDX_HEADS, tm, IDX_DIM), hm),
            pl.BlockSpec((1, tm, IDX_DIM), row),
            pl.BlockSpec((1, tm, HEAD_DIM), row),
            pl.BlockSpec((1, tm // TK, VROWS, TK), lambda i: (i // nst, i % nst, 0, 0)),
            pl.BlockSpec((1, IDX_HEADS, tm), lambda i: (i // nst, 0, i % nst)),
            pl.BlockSpec((1, N_MEM_HEADS, tm, HEAD_DIM), hm),
        ],
        out_shape=[
            jax.ShapeDtypeStruct((B, N_TOK_HEADS, S, HEAD_DIM), BF16),
            jax.ShapeDtypeStruct((B, IDX_HEADS, S, IDX_DIM), BF16),
            jax.ShapeDtypeStruct((B, S, IDX_DIM), BF16),
            jax.ShapeDtypeStruct((B, S, HEAD_DIM), BF16),
            jax.ShapeDtypeStruct((B, S // TK, VROWS, TK), BF16),
            jax.ShapeDtypeStruct((B, IDX_HEADS, S), F32),
            jax.ShapeDtypeStruct((B, N_MEM_HEADS, S, HEAD_DIM), BF16),
        ],
        compiler_params=_cparams(1),
        name="proj_dsa",
    )(x2d, g.reshape(1, D), w, gkv.reshape(1, DSA_KV_RANK), wkv, *tabs)
    return outs


def _proj_nsa_kernel(x_ref, g_ref, w_ref, c_ref, su_ref, sd_ref,
                     qraw_ref, qrot_ref, kc_ref, vc_ref, ks_ref, vs_ref, kw_ref, vw_ref,
                     gt_ref, qm_ref):
    h = _rms(x_ref[...], g_ref[...]).astype(BF16)
    c, su, sd = c_ref[...], su_ref[...], sd_ref[...]

    def mm(lo, hi):
        return jnp.dot(h, w_ref[:, lo:hi], preferred_element_type=F32)

    scale = HEAD_DIM ** -0.5
    q = mm(0, TOK_W)
    _store_heads(qraw_ref, [ch * scale for ch in _chunks(q)], BF16)
    _store_heads(qrot_ref, [ch * SCALE_LOG2 for ch in _rope(q, c, su, sd)], BF16)

    o = TOK_W
    _store_heads(kc_ref, _chunks(mm(o, o + LANE)), F32)
    _store_heads(vc_ref, _chunks(mm(o + LANE, o + 2 * LANE)), F32)
    _store_heads(ks_ref, _rope(mm(o + 2 * LANE, o + 3 * LANE), c, su, sd), BF16)
    _store_values_t(vs_ref, mm(o + 3 * LANE, o + 4 * LANE), 0, NSA_GROUPS)
    _store_heads(kw_ref, _rope(mm(o + 4 * LANE, o + 5 * LANE), c, su, sd), BF16)
    _store_values_t(vw_ref, mm(o + 5 * LANE, o + 6 * LANE), 0, NSA_GROUPS)

    gates = jax.nn.sigmoid(mm(o + 6 * LANE, o + 7 * LANE))
    gt_ref[0] = gates.T[:GATE_ROWS, :]

    qm = mm(o + 7 * LANE, o + 7 * LANE + MEM_Q)
    _store_heads(qm_ref, [ch * SCALE_LOG2 for ch in _chunks(qm)], BF16)


def _proj_nsa(x2d, g, w_in, tabs, B, S):
    D = x2d.shape[1]
    kv_w = NSA_GROUPS * HEAD_DIM
    o_g = TOK_W + 6 * kv_w
    pad = jnp.zeros((D, LANE - N_GATES), w_in.dtype)
    w = jnp.concatenate([w_in[:, :o_g + N_GATES], pad, w_in[:, o_g + N_GATES:]], axis=1).astype(BF16)
    ncol = w.shape[1]
    tm = min(TM_PROJ, S)
    nst = S // tm
    tok = lambda i: (i, 0)
    const = lambda i: (0, 0)
    tab = lambda i: (i % nst, 0)
    hm = lambda i: (i // nst, 0, i % nst, 0)

    def hm_spec(nh):
        return pl.BlockSpec((1, nh, tm, HEAD_DIM), hm)

    def hm_shape(nh, dt):
        return jax.ShapeDtypeStruct((B, nh, S, HEAD_DIM), dt)

    G = NSA_GROUPS
    vt_spec = pl.BlockSpec((1, tm // TK, G * VROWS, TK), lambda i: (i // nst, i % nst, 0, 0))
    vt_shape = jax.ShapeDtypeStruct((B, S // TK, G * VROWS, TK), BF16)
    outs = pl.pallas_call(
        _proj_nsa_kernel,
        grid=(B * nst,),
        in_specs=[
            pl.BlockSpec((tm, D), tok),
            pl.BlockSpec((1, D), const),
            pl.BlockSpec((D, ncol), const),
            pl.BlockSpec((tm, LANE), tab),
            pl.BlockSpec((tm, LANE), tab),
            pl.BlockSpec((tm, LANE), tab),
        ],
        out_specs=[
            hm_spec(N_TOK_HEADS), hm_spec(N_TOK_HEADS),
            hm_spec(G), hm_spec(G), hm_spec(G), vt_spec, hm_spec(G), vt_spec,
            pl.BlockSpec((1, GATE_ROWS, tm), lambda i: (i // nst, 0, i % nst)),
            hm_spec(N_MEM_HEADS),
        ],
        out_shape=[
            hm_shape(N_TOK_HEADS, BF16), hm_shape(N_TOK_HEADS, BF16),
            hm_shape(G, F32), hm_shape(G, F32),
            hm_shape(G, BF16), vt_shape, hm_shape(G, BF16), vt_shape,
            jax.ShapeDtypeStruct((B, GATE_ROWS, S), F32),
            hm_shape(N_MEM_HEADS, BF16),
        ],
        compiler_params=_cparams(1),
        name="proj_nsa",
    )(x2d, g.reshape(1, D), w, *tabs)
    return outs


def _mem_kv_kernel(m_ref, g_ref, w_ref, k_ref, vt_ref):
    h = _rms(m_ref[0], g_ref[...]).astype(BF16)
    kv = jnp.dot(h, w_ref[...], preferred_element_type=F32)
    _store_heads(k_ref, _chunks(kv[:, :MEM_Q]), BF16)
    ones = jnp.ones((ONES_ROWS, kv.shape[0]), BF16)
    for j in range(MEM_Q // LANE):
        pair_t = kv[:, MEM_Q + j * LANE:MEM_Q + (j + 1) * LANE].T.astype(BF16)
        for r in range(LANE // HEAD_DIM):
            h = j * (LANE // HEAD_DIM) + r
            vt_ref[0, h * VROWS:h * VROWS + HEAD_DIM, :] = pair_t[r * HEAD_DIM:(r + 1) * HEAD_DIM, :]
            vt_ref[0, h * VROWS + HEAD_DIM:(h + 1) * VROWS, :] = ones


def _mem_kv(mem, g, w_kv):
    B, NM, D = mem.shape
    assert NM % LANE == 0
    spec = pl.BlockSpec((1, N_MEM_HEADS, NM, HEAD_DIM), lambda b: (b, 0, 0, 0))
    shape = jax.ShapeDtypeStruct((B, N_MEM_HEADS, NM, HEAD_DIM), BF16)
    vt_spec = pl.BlockSpec((1, N_MEM_HEADS * VROWS, NM), lambda b: (b, 0, 0))
    vt_shape = jax.ShapeDtypeStruct((B, N_MEM_HEADS * VROWS, NM), BF16)
    return pl.pallas_call(
        _mem_kv_kernel,
        grid=(B,),
        in_specs=[
            pl.BlockSpec((1, NM, D), lambda b: (b, 0, 0)),
            pl.BlockSpec((1, D), lambda b: (0, 0)),
            pl.BlockSpec((D, 2 * MEM_Q), lambda b: (0, 0)),
        ],
        out_specs=[spec, vt_spec],
        out_shape=[shape, vt_shape],
        compiler_params=_cparams(1),
        name="mem_kv",
    )(mem, g.reshape(1, D), w_kv.astype(BF16))


def _write_pair_t(out_ref, col, o_a, o_b):
    pair = jnp.concatenate([o_a, o_b], axis=0)
    out_ref[0, :, col:col + 2 * HEAD_DIM] = pair.T.astype(out_ref.dtype)


def _mem_attention_t(qm_ref, km_ref, vmt_ref, out_ref):
    outs = []
    for h in range(N_MEM_HEADS):
        s = lax.dot_general(km_ref[0, h], qm_ref[0, h], _NT, preferred_element_type=F32)
        m = jnp.max(s, axis=0, keepdims=True)
        p = jnp.exp2(s - m)
        vt = vmt_ref[0, h * VROWS:(h + 1) * VROWS, :]
        outs.append(_normalised(jnp.dot(vt, p.astype(BF16), preferred_element_type=F32)))
    for h in range(0, N_MEM_HEADS, 2):
        _write_pair_t(out_ref, TOK_W + h * HEAD_DIM, outs[h], outs[h + 1])


def _normalised(acc):
    return acc[:HEAD_DIM] / acc[HEAD_DIM:HEAD_DIM + 1]


def _attend_groups(groups, q_ref, lo, hi, m_s, acc_s):
    heads = [h for (h0, nh, _, _, _) in groups for h in range(h0, h0 + nh)]
    neg = jnp.full((1, TQ), NEG, BF16).astype(F32)
    for h in heads:
        m_s[h] = neg
        acc_s[h] = jnp.zeros((VROWS, TQ), F32)

    def body(kc, _):
        scores = []
        for (h0, nh, k_chunk, _, _) in groups:
            q_g = q_ref[0, h0:h0 + nh].reshape(nh * TQ, HEAD_DIM)
            scores.append(lax.dot_general(k_chunk(kc), q_g, _NT, preferred_element_type=F32))
        for (h0, nh, _, vt_chunk, bias_chunk), s_g in zip(groups, scores):
            for h in range(h0, h0 + nh):
                s = s_g[:, (h - h0) * TQ:(h - h0 + 1) * TQ].astype(BF16) + bias_chunk(kc)
                m = m_s[h]
                m_new = jnp.maximum(m, jnp.max(s, axis=0, keepdims=True).astype(F32))
                m_s[h] = m_new
                p = jnp.exp2(s - m_new.astype(BF16))
                pv = jnp.dot(vt_chunk(kc), p, preferred_element_type=F32)
                acc_s[h] = jnp.exp2(m - m_new) * acc_s[h] + pv
        return 0

    lax.fori_loop(lo, hi, body, 0)
    for h in heads:
        acc_s[h, :HEAD_DIM, :] = _normalised(acc_s[h])


def _dsa_attn_kernel(q_ref, qi_ref, wt_ref, kidx_ref, k_ref, vt_ref, qm_ref, km_ref, vmt_ref,
                     out_ref, key_s, top_s, bias_s, m_s, o_s, *, topk):
    i = pl.program_id(1)
    nk = ((i + 1) * TQ + TK - 1) // TK
    t_pos = i * TQ + lax.broadcasted_iota(jnp.int32, (TK, TQ), 1)
    row = lax.broadcasted_iota(jnp.int32, (TK, TQ), 0)
    wt = wt_ref[0]

    def rows(kc):
        return pl.ds(pl.multiple_of(kc * TK, TK), TK)

    def score_body(kc, _):
        kx = kidx_ref[0, rows(kc), :]
        sc = jnp.zeros((TK, TQ), F32)
        for h in range(IDX_HEADS):
            lg = lax.dot_general(kx, qi_ref[0, h], _NT, preferred_element_type=F32)
            sc = sc + jnp.maximum(lg, 0.0) * wt[h:h + 1, :]
        sc = jnp.where(sc == 0.0, 0.0, sc)
        bits = pltpu.bitcast(sc, jnp.int32)
        key = bits ^ ((bits >> 31) & 0x7FFFFFFF)
        valid = kc * TK + row <= t_pos
        key_s[rows(kc), :] = jnp.where(valid, key, INT_MIN)
        top = pltpu.bitcast(bits & -65536, F32)
        top_s[rows(kc), :] = jnp.where(valid, top, -jnp.inf).astype(BF16)
        return 0

    lax.fori_loop(0, nk, score_body, 0)

    n_acc = 4 * 8

    def count(pred_fn):
        def body(kc, c):
            hit = jnp.where(pred_fn(key_s[rows(kc), :]), 1.0, 0.0)
            return c + jnp.sum(hit.reshape(TK // n_acc, n_acc, TQ), axis=0)

        c = lax.fori_loop(0, nk, body, jnp.zeros((n_acc, TQ), F32))
        return jnp.sum(c, axis=0, keepdims=True)

    def count_top(cand):
        one, zero = jnp.ones((), BF16), jnp.zeros((), BF16)

        def body(kc, c):
            hit = jnp.where(top_s[rows(kc), :] >= cand, one, zero)
            parts = [hit[r * n_acc:(r + 1) * n_acc] for r in range(TK // n_acc)]
            while len(parts) > 1:
                parts = [a + b for a, b in zip(parts[0::2], parts[1::2])]
            return c + parts[0].astype(F32)

        c = lax.fori_loop(0, nk, body, jnp.zeros((n_acc, TQ), F32))
        return jnp.sum(c, axis=0, keepdims=True)

    def top_bit_body(b, carry):
        ans, n_ge = carry
        cand = ans + jnp.left_shift(jnp.int32(1), 15 - b)
        cand_bits = jnp.left_shift(cand ^ ((cand >> 31) & 0x7FFF), 16)
        cnt = count_top(pltpu.bitcast(cand_bits, F32).astype(BF16))
        take = cnt >= topk
        return jnp.where(take, cand, ans), jnp.where(take, cnt, n_ge)

    def bit_body(b, carry):
        ans, n_ge = carry
        cand = ans + jnp.left_shift(jnp.int32(1), 31 - b)
        cnt = count(lambda keys: keys >= cand)
        take = cnt >= topk
        return jnp.where(take, cand, ans), jnp.where(take, cnt, n_ge)

    top, n_ge = lax.fori_loop(
        0, 16, top_bit_body,
        (jnp.full((1, TQ), -2 ** 15, jnp.int32), jnp.full((1, TQ), float(topk), F32)))
    thr, n_ge = lax.fori_loop(16, 32, bit_body, (jnp.left_shift(top, 16), n_ge))
    has_ties = jnp.max(jnp.where((n_ge > topk) & (thr > INT_MIN), 1.0, 0.0)) > 0.0

    @pl.when(jnp.logical_not(has_ties))
    def _():
        def bias_body(kc, _):
            sel = (key_s[rows(kc), :] >= thr) & (kc * TK + row <= t_pos)
            bias_s[rows(kc), :] = jnp.where(sel, 0.0, NEG).astype(BF16)
            return 0

        lax.fori_loop(0, nk, bias_body, 0)

    @pl.when(has_ties)
    def _():
        need = topk - count(lambda keys: keys > thr)
        tri = (lax.broadcasted_iota(jnp.int32, (TK, TK), 1)
               < lax.broadcasted_iota(jnp.int32, (TK, TK), 0)).astype(BF16)

        def bias_body(kc, eq_seen):
            keys = key_s[rows(kc), :]
            eq = keys == thr
            eq_f = jnp.where(eq, 1.0, 0.0)
            before = eq_seen + jnp.dot(tri, eq_f.astype(BF16), preferred_element_type=F32)
            sel = (keys > thr) | (eq & (before < need))
            sel = sel & (kc * TK + row <= t_pos)
            bias_s[rows(kc), :] = jnp.where(sel, 0.0, NEG).astype(BF16)
            return eq_seen + jnp.sum(eq_f, axis=0, keepdims=True)

        lax.fori_loop(0, nk, bias_body, jnp.zeros((1, TQ), F32))

    _attend_groups(
        [(0, N_TOK_HEADS,
          lambda kc: k_ref[0, rows(kc), :],
          lambda kc: vt_ref[0, kc],
          lambda kc: bias_s[rows(kc), :])],
        q_ref, 0, nk, m_s, o_s)
    for h in range(0, N_TOK_HEADS, 2):
        _write_pair_t(out_ref, h * HEAD_DIM, o_s[h, :HEAD_DIM, :], o_s[h + 1, :HEAD_DIM, :])
    _mem_attention_t(qm_ref, km_ref, vmt_ref, out_ref)


def _dsa_attn(q, qi, wt, kidx, k, vt, qm, km, vmt, B, S):
    NM = km.shape[2]
    topk = min(DSA_TOPK_MAX, S // 4)
    assert S % TK == 0
    key_rows = S
    stat = pltpu.VMEM((N_TOK_HEADS, 1, TQ), F32)
    qblk = lambda nh: pl.BlockSpec((1, nh, TQ, HEAD_DIM), lambda b, i: (b, 0, i, 0))
    full = pl.BlockSpec((1, S, HEAD_DIM), lambda b, i: (b, 0, 0))
    return pl.pallas_call(
        functools.partial(_dsa_attn_kernel, topk=topk),
        grid=(B, S // TQ),
        in_specs=[
            qblk(N_TOK_HEADS), qblk(IDX_HEADS),
            pl.BlockSpec((1, IDX_HEADS, TQ), lambda b, i: (b, 0, i)),
            full, full,
            pl.BlockSpec((1, S // TK, VROWS, TK), lambda b, i: (b, 0, 0, 0)),
            qblk(N_MEM_HEADS),
            pl.BlockSpec((1, N_MEM_HEADS, NM, HEAD_DIM), lambda b, i: (b, 0, 0, 0)),
            pl.BlockSpec((1, N_MEM_HEADS * VROWS, NM), lambda b, i: (b, 0, 0)),
        ],
        out_specs=pl.BlockSpec((1, TQ, TOK_W + MEM_Q), lambda b, i: (b, i, 0)),
        out_shape=jax.ShapeDtypeStruct((B, S, TOK_W + MEM_Q), BF16),
        scratch_shapes=[
            pltpu.VMEM((key_rows, TQ), jnp.int32),
            pltpu.VMEM((key_rows, TQ), BF16),
            pltpu.VMEM((S, TQ), BF16),
            stat,
            pltpu.VMEM((N_TOK_HEADS, VROWS, TQ), F32),
        ],
        compiler_params=_cparams(2),
        name="dsa_attn",
    )(q, qi, wt, kidx, k, vt, qm, km, vmt)


def _nsa_compress_kernel(kc_ref, vc_ref, pk_ref, pv_ref, kw1_ref, kw2_ref, vw1_ref, vw2_ref,
                         ko_ref, vo_ref):
    half = (CMP_LEN // 2) * HEAD_DIM

    def run(x_ref, pos_ref, w1_ref, w2_ref, o_ref):
        r = x_ref[0, 0]
        nxt = pltpu.roll(r, r.shape[0] - 1, 0)
        a = (r + pos_ref[0:1, :]).astype(BF16)
        b = (nxt + pos_ref[1:2, :]).astype(BF16)
        hid = (jnp.dot(a, w1_ref[:half, :], preferred_element_type=F32)
               + jnp.dot(b, w1_ref[half:, :], preferred_element_type=F32))
        hid = jax.nn.gelu(hid).astype(BF16)
        o_ref[0, 0] = jnp.dot(hid, w2_ref[...], preferred_element_type=F32).astype(o_ref.dtype)

    run(kc_ref, pk_ref, kw1_ref, kw2_ref, ko_ref)
    run(vc_ref, pv_ref, vw1_ref, vw2_ref, vo_ref)


def _nsa_compress(kc, vc, pos_k, pos_v, k_w1, k_w2, v_w1, v_w2, B, S):
    G = NSA_GROUPS
    R = S // CMP_STRIDE
    W = CMP_STRIDE * HEAD_DIM
    kc = kc.reshape(B, G, R, W)
    vc = vc.reshape(B, G, R, W)
    xs = pl.BlockSpec((1, 1, R, W), lambda b, g: (b, g, 0, 0))
    cst = lambda shp: pl.BlockSpec(shp, lambda b, g: (0, 0))
    osz = pl.BlockSpec((1, 1, R, HEAD_DIM), lambda b, g: (b, g, 0, 0))
    osh = jax.ShapeDtypeStruct((B, G, R, HEAD_DIM), BF16)
    return pl.pallas_call(
        _nsa_compress_kernel,
        grid=(B, G),
        in_specs=[xs, xs, cst((2, W)), cst((2, W)),
                  cst((CMP_LEN * HEAD_DIM, CMP_HIDDEN)), cst((CMP_HIDDEN, HEAD_DIM)),
                  cst((CMP_LEN * HEAD_DIM, CMP_HIDDEN)), cst((CMP_HIDDEN, HEAD_DIM))],
        out_specs=[osz, osz],
        out_shape=[osh, osh],
        compiler_params=_cparams(2),
        name="nsa_compress",
    )(kc, vc, pos_k.reshape(2, W), pos_v.reshape(2, W),
      k_w1.astype(BF16), k_w2.astype(BF16), v_w1.astype(BF16), v_w2.astype(BF16))


def _nsa_attn_kernel(qraw_ref, qrot_ref, kcmp_ref, vcmp_ref, ks_ref, vst_ref, kw_ref, vwt_ref,
                     gt_ref, ovt_ref, qm_ref, km_ref, vmt_ref, out_ref,
                     sel_s, bsel_s, wb_s, m_s, oc_s, os_s, ow_s, *, n_cmp, n_slc, n_sel):
    i = pl.program_id(1)
    J = NSA_HPG
    L = J * TQ
    NC = kcmp_ref.shape[2]
    NSP = sel_s.shape[1]
    t_row = i * TQ + lax.broadcasted_iota(jnp.int32, (1, TQ), 1)
    t_pos = i * TQ + lax.broadcasted_iota(jnp.int32, (TK, TQ), 1)
    row = lax.broadcasted_iota(jnp.int32, (TK, TQ), 0)
    gt = gt_ref[0]

    c_idx = lax.broadcasted_iota(jnp.int32, (NC, TQ), 0)
    t_c = i * TQ + lax.broadcasted_iota(jnp.int32, (NC, TQ), 1)
    mask_c1 = (c_idx * CMP_STRIDE + CMP_LEN - 1 <= t_c) & (c_idx < n_cmp)
    mask_c = jnp.concatenate([mask_c1] * J, axis=1)

    n_idx = lax.broadcasted_iota(jnp.int32, (NSP, TQ), 0)
    cur = t_row // SLC_LEN
    forced = (n_idx == 0) | (n_idx == cur) | (n_idx == cur - 1)
    admissible = (n_idx <= cur) & (n_idx < n_slc)

    for g in range(NSA_GROUPS):
        h0 = g * J
        q_raw = qraw_ref[0, h0:h0 + J].reshape(L, HEAD_DIM)
        s_c = lax.dot_general(kcmp_ref[0, g], q_raw, _NT, preferred_element_type=F32)
        s_c = jnp.where(mask_c, s_c, -jnp.inf)
        m_c = jnp.max(s_c, axis=0, keepdims=True)
        m_c = jnp.where(m_c > -jnp.inf, m_c, 0.0)
        p_c = jnp.exp(s_c - m_c)
        p_c = p_c / jnp.maximum(jnp.sum(p_c, axis=0, keepdims=True), 1e-30)
        o_c = lax.dot_general(vcmp_ref[0, g], p_c.astype(BF16), _TN, preferred_element_type=F32)

        p_sum = p_c[:, 0:TQ]
        for j in range(1, J):
            p_sum = p_sum + p_c[:, j * TQ:(j + 1) * TQ]
        imp = jnp.dot(ovt_ref[...], p_sum, preferred_element_type=F32,
                      precision=lax.Precision.HIGHEST)
        imp = jnp.where(forced, FORCE_SCORE, imp)
        imp = jnp.where(admissible, imp, -jnp.inf)
        rank = jnp.zeros((NSP, TQ), F32)
        for mrow in range(n_slc):
            other = imp[mrow:mrow + 1, :]
            ahead = (other > imp) | ((other == imp) & (n_idx > mrow))
            rank = rank + jnp.where(ahead, 1.0, 0.0)
        sel_s[g] = jnp.where(rank < n_sel, 1.0, 0.0)
        for j in range(J):
            oc_s[h0 + j] = o_c[:, j * TQ:(j + 1) * TQ]

    def rows(kc):
        return pl.ds(pl.multiple_of(kc * TK, TK), TK)

    tiles_per_chunk = TK // TQ
    far = WIN // TQ + 1
    rq = lax.broadcasted_iota(jnp.int32, (TQ, TQ), 0)
    cq = lax.broadcasted_iota(jnp.int32, (TQ, TQ), 1)
    for t in range(far + tiles_per_chunk):
        d = far - t
        ok = (rq <= cq + d * TQ) & (rq > cq - WIN + d * TQ)
        wb_s[t * TQ:(t + 1) * TQ, :] = jnp.where(ok, 0.0, NEG).astype(BF16)

    def win_bias(kc):
        t0 = far - i + kc * tiles_per_chunk
        return wb_s[pl.ds(pl.multiple_of(t0 * TQ, TQ), TK), :]

    nk = ((i + 1) * TQ + TK - 1) // TK
    lo = (jnp.maximum(i - WIN // TQ, 0) * TQ) // TK
    per = TK // SLC_LEN
    for g in range(NSA_GROUPS):
        def bias_body(kc, _, g=g):
            picked = jnp.concatenate(
                [jnp.broadcast_to(sel_s[g, pl.ds(kc * per + r, 1), :], (SLC_LEN, TQ))
                 for r in range(per)], axis=0) > 0.5
            ok = picked & (kc * TK + row <= t_pos)
            bsel_s[g, rows(kc), :] = jnp.where(ok, 0.0, NEG).astype(BF16)
            return 0

        lax.fori_loop(0, nk, bias_body, 0)

    def group_spec(g, k_ref_, vt_ref_, bias_chunk):
        return (g * J, J,
                lambda kc: k_ref_[0, g, rows(kc), :],
                lambda kc: vt_ref_[0, kc, g * VROWS:(g + 1) * VROWS, :],
                bias_chunk)

    _attend_groups(
        [group_spec(g, ks_ref, vst_ref, lambda kc, g=g: bsel_s[g, rows(kc), :])
         for g in range(NSA_GROUPS)],
        qrot_ref, 0, nk, m_s, os_s)
    _attend_groups(
        [group_spec(g, kw_ref, vwt_ref, win_bias) for g in range(NSA_GROUPS)],
        qrot_ref, lo, nk, m_s, ow_s)

    def gated(h):
        return (gt[3 * h:3 * h + 1, :] * oc_s[h] + gt[3 * h + 1:3 * h + 2, :] * os_s[h, :HEAD_DIM, :]
                + gt[3 * h + 2:3 * h + 3, :] * ow_s[h, :HEAD_DIM, :])

    for h in range(0, N_TOK_HEADS, 2):
        _write_pair_t(out_ref, h * HEAD_DIM, gated(h), gated(h + 1))
    _mem_attention_t(qm_ref, km_ref, vmt_ref, out_ref)


def _nsa_attn(qraw, qrot, kcmp, vcmp, ks, vst, kw, vwt, gt, qm, km, vmt, B, S):
    G = NSA_GROUPS
    NM = km.shape[2]
    NC = kcmp.shape[2]
    n_cmp = (S - CMP_LEN) // CMP_STRIDE + 1
    n_slc = S // SLC_LEN
    n_sel = min(SLC_TOP_MAX, n_slc)
    nsp = -(-n_slc // 8) * 8
    c0 = np.arange(NC) * CMP_STRIDE
    s0 = np.arange(nsp) * SLC_LEN
    ov = np.minimum(c0[None, :] + CMP_LEN, s0[:, None] + SLC_LEN) - np.maximum(c0[None, :], s0[:, None])
    ovt = (np.clip(ov, 0, None) / CMP_LEN).astype(np.float32)
    ovt[:, n_cmp:] = 0.0
    ovt[n_slc:, :] = 0.0

    qblk = lambda nh: pl.BlockSpec((1, nh, TQ, HEAD_DIM), lambda b, i: (b, 0, i, 0))
    full = pl.BlockSpec((1, G, S, HEAD_DIM), lambda b, i: (b, 0, 0, 0))
    full_t = pl.BlockSpec((1, S // TK, G * VROWS, TK), lambda b, i: (b, 0, 0, 0))
    cmpspec = pl.BlockSpec((1, G, NC, HEAD_DIM), lambda b, i: (b, 0, 0, 0))
    head_out = pltpu.VMEM((N_TOK_HEADS, HEAD_DIM, TQ), F32)
    head_acc = pltpu.VMEM((N_TOK_HEADS, VROWS, TQ), F32)
    stat = pltpu.VMEM((N_TOK_HEADS, 1, TQ), F32)
    return pl.pallas_call(
        functools.partial(_nsa_attn_kernel, n_cmp=n_cmp, n_slc=n_slc, n_sel=n_sel),
        grid=(B, S // TQ),
        in_specs=[
            qblk(N_TOK_HEADS), qblk(N_TOK_HEADS), cmpspec, cmpspec, full, full_t, full, full_t,
            pl.BlockSpec((1, GATE_ROWS, TQ), lambda b, i: (b, 0, i)),
            pl.BlockSpec((nsp, NC), lambda b, i: (0, 0)),
            qblk(N_MEM_HEADS),
            pl.BlockSpec((1, N_MEM_HEADS, NM, HEAD_DIM), lambda b, i: (b, 0, 0, 0)),
            pl.BlockSpec((1, N_MEM_HEADS * VROWS, NM), lambda b, i: (b, 0, 0)),
        ],
        out_specs=pl.BlockSpec((1, TQ, TOK_W + MEM_Q), lambda b, i: (b, i, 0)),
        out_shape=jax.ShapeDtypeStruct((B, S, TOK_W + MEM_Q), BF16),
        scratch_shapes=[
            pltpu.VMEM((G, nsp, TQ), F32),
            pltpu.VMEM((G, S, TQ), BF16),
            pltpu.VMEM(((WIN // TQ + 1 + TK // TQ) * TQ, TQ), BF16),
            stat,
            head_out, head_acc, head_acc,
        ],
        compiler_params=_cparams(2),
        name="nsa_attn",
    )(qraw, qrot, kcmp, vcmp, ks, vst, kw, vwt, gt, jnp.asarray(ovt), qm, km, vmt)


def _post_attn_kernel(x_ref, mix_ref, wo_ref, g_ref, win_ref, wdn_ref, gf_ref, out_ref, *, final):
    d_ff = wdn_ref.shape[0]
    x1 = x_ref[...] + jnp.dot(mix_ref[...], wo_ref[...], preferred_element_type=F32)
    h = _rms(x1, g_ref[...]).astype(BF16)
    gate = jnp.dot(h, win_ref[:, :d_ff], preferred_element_type=F32)
    up = jnp.dot(h, win_ref[:, d_ff:], preferred_element_type=F32)
    act = (jax.nn.silu(gate) * up).astype(BF16)
    x2 = x1 + jnp.dot(act, wdn_ref[...], preferred_element_type=F32)
    if final:
        x2 = _rms(x2, gf_ref[...])
    out_ref[...] = x2


def _post_attn(x2d, mix2d, w_o, g, w_in, w_down, g_final, final):
    N, D = x2d.shape
    MW = mix2d.shape[1]
    d_ff = w_down.shape[0]
    tm = min(TM_FFN, N)
    const = lambda shp: pl.BlockSpec(shp, lambda i: (0, 0), pipeline_mode=pl.Buffered(1))
    return pl.pallas_call(
        functools.partial(_post_attn_kernel, final=final),
        grid=(N // tm,),
        in_specs=[
            pl.BlockSpec((tm, D), lambda i: (i, 0)),
            pl.BlockSpec((tm, MW), lambda i: (i, 0)),
            const((MW, D)), const((1, D)), const((D, 2 * d_ff)), const((d_ff, D)), const((1, D)),
        ],
        out_specs=pl.BlockSpec((tm, D), lambda i: (i, 0)),
        out_shape=jax.ShapeDtypeStruct((N, D), F32),
        compiler_params=_cparams(1),
        name="post_attn",
    )(x2d, mix2d, w_o.astype(BF16), g.reshape(1, D), w_in.astype(BF16), w_down.astype(BF16),
      g_final.reshape(1, D))


def kernel(x, mem, attn_norm, mem_norm, ffn_norm, final_norm, dsa_w_in, dsa_ckv_norm, dsa_w_uk, dsa_w_uv, nsa_w_in, nsa_cmp_pos_k, nsa_cmp_pos_v, nsa_cmp_k_w1, nsa_cmp_k_w2, nsa_cmp_v_w1, nsa_cmp_v_w2, mem_w_kv, w_o, ffn_w_in, ffn_w_down):
    B, S, D = x.shape
    depth = attn_norm.shape[0]
    assert S % TM_PROJ == 0 or S < TM_PROJ
    assert S % TQ == 0 and (B * S) % TM_FFN == 0
    tabs = _rope_tables(S)
    x2d = x.reshape(B * S, D)
    for i in range(depth):
        km, vm = _mem_kv(mem, mem_norm[i], mem_w_kv[i])
        if i % 2 == 0:
            a = i // 2
            q, qi, kidx, k, v, wt, qm = _proj_dsa(
                x2d, attn_norm[i], dsa_w_in[a], dsa_ckv_norm[a], dsa_w_uk[a], dsa_w_uv[a], tabs, B, S)
            mix = _dsa_attn(q, qi, wt, kidx, k, v, qm, km, vm, B, S)
        else:
            b = i // 2
            qraw, qrot, kc, vc, ks, vs, kw, vw, gt, qm = _proj_nsa(
                x2d, attn_norm[i], nsa_w_in[b], tabs, B, S)
            kcmp, vcmp = _nsa_compress(
                kc, vc, nsa_cmp_pos_k[b], nsa_cmp_pos_v[b],
                nsa_cmp_k_w1[b], nsa_cmp_k_w2[b], nsa_cmp_v_w1[b], nsa_cmp_v_w2[b], B, S)
            mix = _nsa_attn(qraw, qrot, kcmp, vcmp, ks, vs, kw, vw, gt, qm, km, vm, B, S)
        x2d = _post_attn(x2d, mix.reshape(B * S, TOK_W + MEM_Q), w_o[i], ffn_norm[i],
                         ffn_w_in[i], ffn_w_down[i], final_norm, final=(i == depth - 1))
    return x2d.reshape(B, S, D)
```

```python
import functools
import math

import numpy as np
import jax
import jax.numpy as jnp
from jax import lax
from jax.experimental import pallas as pl
from jax.experimental.pallas import tpu as pltpu

F32 = jnp.float32
BF16 = jnp.bfloat16

HEAD_DIM = 64
ROPE_DIM = 16
ROPE_THETA = 500000.0
RMS_EPS = 1e-6
N_TOK_HEADS = 12
N_MEM_HEADS = 4
TOK_W = N_TOK_HEADS * HEAD_DIM
MEM_Q = N_MEM_HEADS * HEAD_DIM
DSA_KV_RANK = 128
DSA_NOPE = HEAD_DIM - ROPE_DIM
IDX_HEADS = 8
IDX_DIM = 64
DSA_TOPK_MAX = 256
NSA_GROUPS = 2
NSA_HPG = N_TOK_HEADS // NSA_GROUPS
CMP_LEN = 32
CMP_STRIDE = 16
CMP_HIDDEN = 128
SLC_LEN = 64
SLC_TOP_MAX = 16
WIN = 512
FORCE_SCORE = 1e9
N_GATES = N_TOK_HEADS * 3
GATE_ROWS = 40

LANE = 128
TQ = 256
TK = 256
ONES_ROWS = 16
VROWS = HEAD_DIM + ONES_ROWS
TM_PROJ = 512
TM_FFN = 512
VMEM_LIMIT = 56 * 1024 * 1024
NEG = -1e30
SCALE_LOG2 = HEAD_DIM ** -0.5 * math.log2(math.e)
INT_MIN = -2 ** 31

_NT = (((1,), (1,)), ((), ()))
_TN = (((0,), (0,)), ((), ()))


def _cparams(n_axes):
    return pltpu.CompilerParams(
        dimension_semantics=("arbitrary",) * n_axes, vmem_limit_bytes=VMEM_LIMIT)


def _rms(xf, g):
    ms = jnp.mean(xf * xf, axis=-1, keepdims=True)
    return xf * lax.rsqrt(ms + RMS_EPS) * g


def _rope_tables(S):
    inv = ROPE_THETA ** (-np.arange(0, ROPE_DIM, 2, dtype=np.float64) / ROPE_DIM)
    ang = np.arange(S, dtype=np.float64)[:, None] * inv[None, :]
    cos = np.cos(ang).astype(np.float32)
    sin = np.sin(ang).astype(np.float32)
    half = ROPE_DIM // 2
    c = np.ones((S, LANE), np.float32)
    s_up = np.zeros((S, LANE), np.float32)
    s_dn = np.zeros((S, LANE), np.float32)
    for base in range(0, LANE, HEAD_DIM):
        c[:, base:base + half] = cos
        c[:, base + half:base + ROPE_DIM] = cos
        s_up[:, base + half:base + ROPE_DIM] = sin
        s_dn[:, base:base + half] = -sin
    return jnp.asarray(c), jnp.asarray(s_up), jnp.asarray(s_dn)


def _rope(x, c, s_up, s_dn):
    half = ROPE_DIM // 2
    outs = []
    for j in range(x.shape[1] // LANE):
        xc = x[:, j * LANE:(j + 1) * LANE]
        outs.append(xc * c + pltpu.roll(xc, half, 1) * s_up + pltpu.roll(xc, LANE - half, 1) * s_dn)
    return outs


def _store_heads(ref, chunks, dtype):
    for j, ch in enumerate(chunks):
        ref[0, 2 * j, :, :] = ch[:, :HEAD_DIM].astype(dtype)
        ref[0, 2 * j + 1, :, :] = ch[:, HEAD_DIM:].astype(dtype)


def _chunks(x):
    return [x[:, j * LANE:(j + 1) * LANE] for j in range(x.shape[1] // LANE)]


def _store_values_t(ref, x, row_lo, n_groups):
    xt = x.T
    ones = jnp.ones((ONES_ROWS, TK), ref.dtype)
    for j in range(x.shape[0] // TK):
        for g in range(n_groups):
            r0 = row_lo + g * HEAD_DIM
            ref[0, j, g * VROWS:g * VROWS + HEAD_DIM, :] = (
                xt[r0:r0 + HEAD_DIM, j * TK:(j + 1) * TK].astype(ref.dtype))
            ref[0, j, g * VROWS + HEAD_DIM:(g + 1) * VROWS, :] = ones


def _proj_dsa_kernel(x_ref, g_ref, w_ref, gkv_ref, wkv_ref, c_ref, su_ref, sd_ref,
                     q_ref, qi_ref, kidx_ref, k_ref, vt_ref, wt_ref, qm_ref):
    h = _rms(x_ref[...], g_ref[...]).astype(BF16)
    c, su, sd = c_ref[...], su_ref[...], sd_ref[...]

    def mm(lo, hi):
        return jnp.dot(h, w_ref[:, lo:hi], preferred_element_type=F32)

    q = _rope(mm(0, TOK_W), c, su, sd)
    _store_heads(q_ref, [ch * SCALE_LOG2 for ch in q], BF16)

    ckv = _rms(mm(TOK_W, TOK_W + DSA_KV_RANK), gkv_ref[...]).astype(BF16)
    kvn = jnp.dot(ckv, wkv_ref[...], preferred_element_type=F32)

    o_qi = TOK_W + DSA_KV_RANK
    qi = _rope(mm(o_qi, o_qi + IDX_HEADS * IDX_DIM), c, su, sd)
    _store_heads(qi_ref, qi, BF16)

    o_misc = o_qi + IDX_HEADS * IDX_DIM
    misc = _rope(mm(o_misc, o_misc + LANE), c, su, sd)[0]
    kidx_ref[0] = misc[:, :IDX_DIM].astype(BF16)
    lane = lax.broadcasted_iota(jnp.int32, misc.shape, 1)
    k_full = jnp.where(lane < ROPE_DIM, pltpu.roll(misc, HEAD_DIM, 1), kvn)
    k_ref[0] = k_full[:, :HEAD_DIM].astype(BF16)
    _store_values_t(vt_ref, kvn, HEAD_DIM, 1)
    w_scaled = misc * (IDX_HEADS ** -0.5 * IDX_DIM ** -0.5)
    w_row0 = IDX_DIM + ROPE_DIM
    wt_ref[0] = w_scaled.T[w_row0:w_row0 + IDX_HEADS, :]

    o_qm = o_misc + LANE
    qm = mm(o_qm, o_qm + MEM_Q)
    _store_heads(qm_ref, [ch * SCALE_LOG2 for ch in _chunks(qm)], BF16)


def _proj_dsa(x2d, g, w_in, gkv, w_uk, w_uv, tabs, B, S):
    D = x2d.shape[1]
    o = np.cumsum([0, TOK_W, DSA_KV_RANK, ROPE_DIM, IDX_HEADS * IDX_DIM, IDX_DIM, IDX_HEADS, MEM_Q])
    q, ckv, kr, qi, ki, wi, qm = [w_in[:, o[i]:o[i + 1]] for i in range(7)]
    pad = jnp.zeros((D, LANE - IDX_DIM - ROPE_DIM - IDX_HEADS), w_in.dtype)
    w = jnp.concatenate([q, ckv, qi, ki, kr, wi, pad, qm], axis=1).astype(BF16)
    wkv = jnp.concatenate([jnp.zeros((DSA_KV_RANK, ROPE_DIM), F32), w_uk, w_uv], axis=1).astype(BF16)
    ncol = w.shape[1]
    tm = min(TM_PROJ, S)
    nst = S // tm
    tok = lambda i: (i, 0)
    const = lambda i: (0, 0)
    tab = lambda i: (i % nst, 0)
    hm = lambda i: (i // nst, 0, i % nst, 0)
    row = lambda i: (i // nst, i % nst, 0)
    outs = pl.pallas_call(
        _proj_dsa_kernel,
        grid=(B * nst,),
        in_specs=[
            pl.BlockSpec((tm, D), tok),
            pl.BlockSpec((1, D), const),
            pl.BlockSpec((D, ncol), const),
            pl.BlockSpec((1, DSA_KV_RANK), const),
            pl.BlockSpec((DSA_KV_RANK, LANE), const),
            pl.BlockSpec((tm, LANE), tab),
            pl.BlockSpec((tm, LANE), tab),
            pl.BlockSpec((tm, LANE), tab),
        ],
        out_specs=[
            pl.BlockSpec((1, N_TOK_HEADS, tm, HEAD_DIM), hm),
            pl.BlockSpec((1, IDX_HEADS, tm, IDX_DIM), hm),
            pl.BlockSpec((1, tm, IDX_DIM), row),
            pl.BlockSpec((1, tm, HEAD_DIM), row),
            pl.BlockSpec((1, tm // TK, VROWS, TK), lambda i: (i // nst, i % nst, 0, 0)),
            pl.BlockSpec((1, IDX_HEADS, tm), lambda i: (i // nst, 0, i % nst)),
            pl.BlockSpec((1, N_MEM_HEADS, tm, HEAD_DIM), hm),
        ],
        out_shape=[
            jax.ShapeDtypeStruct((B, N_TOK_HEADS, S, HEAD_DIM), BF16),
            jax.ShapeDtypeStruct((B, IDX_HEADS, S, IDX_DIM), BF16),
            jax.ShapeDtypeStruct((B, S, IDX_DIM), BF16),
            jax.ShapeDtypeStruct((B, S, HEAD_DIM), BF16),
            jax.ShapeDtypeStruct((B, S // TK, VROWS, TK), BF16),
            jax.ShapeDtypeStruct((B, IDX_HEADS, S), F32),
            jax.ShapeDtypeStruct((B, N_MEM_HEADS, S, HEAD_DIM), BF16),
        ],
        compiler_params=_cparams(1),
        name="proj_dsa",
    )(x2d, g.reshape(1, D), w, gkv.reshape(1, DSA_KV_RANK), wkv, *tabs)
    return outs


def _proj_nsa_kernel(x_ref, g_ref, w_ref, c_ref, su_ref, sd_ref,
                     qraw_ref, qrot_ref, kc_ref, vc_ref, ks_ref, vs_ref, kw_ref, vw_ref,
                     gt_ref, qm_ref):
    h = _rms(x_ref[...], g_ref[...]).astype(BF16)
    c, su, sd = c_ref[...], su_ref[...], sd_ref[...]

    def mm(lo, hi):
        return jnp.dot(h, w_ref[:, lo:hi], preferred_element_type=F32)

    q = mm(0, TOK_W)
    _store_heads(qraw_ref, [ch * SCALE_LOG2 for ch in _chunks(q)], BF16)
    _store_heads(qrot_ref, [ch * SCALE_LOG2 for ch in _rope(q, c, su, sd)], BF16)

    o = TOK_W
    _store_heads(kc_ref, _chunks(mm(o, o + LANE)), F32)
    _store_heads(vc_ref, _chunks(mm(o + LANE, o + 2 * LANE)), F32)
    _store_heads(ks_ref, _rope(mm(o + 2 * LANE, o + 3 * LANE), c, su, sd), BF16)
    _store_values_t(vs_ref, mm(o + 3 * LANE, o + 4 * LANE), 0, NSA_GROUPS)
    _store_heads(kw_ref, _rope(mm(o + 4 * LANE, o + 5 * LANE), c, su, sd), BF16)
    _store_values_t(vw_ref, mm(o + 5 * LANE, o + 6 * LANE), 0, NSA_GROUPS)

    gates = jax.nn.sigmoid(mm(o + 6 * LANE, o + 7 * LANE))
    gt_ref[0] = gates.T[:GATE_ROWS, :]

    qm = mm(o + 7 * LANE, o + 7 * LANE + MEM_Q)
    _store_heads(qm_ref, [ch * SCALE_LOG2 for ch in _chunks(qm)], BF16)


def _proj_nsa(x2d, g, w_in, tabs, B, S):
    D = x2d.shape[1]
    kv_w = NSA_GROUPS * HEAD_DIM
    o_g = TOK_W + 6 * kv_w
    pad = jnp.zeros((D, LANE - N_GATES), w_in.dtype)
    w = jnp.concatenate([w_in[:, :o_g + N_GATES], pad, w_in[:, o_g + N_GATES:]], axis=1).astype(BF16)
    ncol = w.shape[1]
    tm = min(TM_PROJ, S)
    nst = S // tm
    tok = lambda i: (i, 0)
    const = lambda i: (0, 0)
    tab = lambda i: (i % nst, 0)
    hm = lambda i: (i // nst, 0, i % nst, 0)

    def hm_spec(nh):
        return pl.BlockSpec((1, nh, tm, HEAD_DIM), hm)

    def hm_shape(nh, dt):
        return jax.ShapeDtypeStruct((B, nh, S, HEAD_DIM), dt)

    G = NSA_GROUPS
    vt_spec = pl.BlockSpec((1, tm // TK, G * VROWS, TK), lambda i: (i // nst, i % nst, 0, 0))
    vt_shape = jax.ShapeDtypeStruct((B, S // TK, G * VROWS, TK), BF16)
    outs = pl.pallas_call(
        _proj_nsa_kernel,
        grid=(B * nst,),
        in_specs=[
            pl.BlockSpec((tm, D), tok),
            pl.BlockSpec((1, D), const),
            pl.BlockSpec((D, ncol), const),
            pl.BlockSpec((tm, LANE), tab),
            pl.BlockSpec((tm, LANE), tab),
            pl.BlockSpec((tm, LANE), tab),
        ],
        out_specs=[
            hm_spec(N_TOK_HEADS), hm_spec(N_TOK_HEADS),
            hm_spec(G), hm_spec(G), hm_spec(G), vt_spec, hm_spec(G), vt_spec,
            pl.BlockSpec((1, GATE_ROWS, tm), lambda i: (i // nst, 0, i % nst)),
            hm_spec(N_MEM_HEADS),
        ],
        out_shape=[
            hm_shape(N_TOK_HEADS, BF16), hm_shape(N_TOK_HEADS, BF16),
            hm_shape(G, F32), hm_shape(G, F32),
            hm_shape(G, BF16), vt_shape, hm_shape(G, BF16), vt_shape,
            jax.ShapeDtypeStruct((B, GATE_ROWS, S), F32),
            hm_shape(N_MEM_HEADS, BF16),
        ],
        compiler_params=_cparams(1),
        name="proj_nsa",
    )(x2d, g.reshape(1, D), w, *tabs)
    return outs


def _mem_kv_kernel(m_ref, g_ref, w_ref, k_ref, vt_ref):
    h = _rms(m_ref[0], g_ref[...]).astype(BF16)
    kv = jnp.dot(h, w_ref[...], preferred_element_type=F32)
    _store_heads(k_ref, _chunks(kv[:, :MEM_Q]), BF16)
    ones = jnp.ones((ONES_ROWS, kv.shape[0]), BF16)
    for j in range(MEM_Q // LANE):
        pair_t = kv[:, MEM_Q + j * LANE:MEM_Q + (j + 1) * LANE].T.astype(BF16)
        for r in range(LANE // HEAD_DIM):
            h = j * (LANE // HEAD_DIM) + r
            vt_ref[0, h * VROWS:h * VROWS + HEAD_DIM, :] = pair_t[r * HEAD_DIM:(r + 1) * HEAD_DIM, :]
            vt_ref[0, h * VROWS + HEAD_DIM:(h + 1) * VROWS, :] = ones


def _mem_kv(mem, g, w_kv):
    B, NM, D = mem.shape
    assert NM % LANE == 0
    spec = pl.BlockSpec((1, N_MEM_HEADS, NM, HEAD_DIM), lambda b: (b, 0, 0, 0))
    shape = jax.ShapeDtypeStruct((B, N_MEM_HEADS, NM, HEAD_DIM), BF16)
    vt_spec = pl.BlockSpec((1, N_MEM_HEADS * VROWS, NM), lambda b: (b, 0, 0))
    vt_shape = jax.ShapeDtypeStruct((B, N_MEM_HEADS * VROWS, NM), BF16)
    return pl.pallas_call(
        _mem_kv_kernel,
        grid=(B,),
        in_specs=[
            pl.BlockSpec((1, NM, D), lambda b: (b, 0, 0)),
            pl.BlockSpec((1, D), lambda b: (0, 0)),
            pl.BlockSpec((D, 2 * MEM_Q), lambda b: (0, 0)),
        ],
        out_specs=[spec, vt_spec],
        out_shape=[shape, vt_shape],
        compiler_params=_cparams(1),
        name="mem_kv",
    )(mem, g.reshape(1, D), w_kv.astype(BF16))


def _write_pair_t(out_ref, col, o_a, o_b):
    pair = jnp.concatenate([o_a, o_b], axis=0)
    out_ref[0, :, col:col + 2 * HEAD_DIM] = pair.T.astype(out_ref.dtype)


def _mem_attention_t(qm_ref, km_ref, vmt_ref, out_ref):
    scores = [lax.dot_general(km_ref[0, h], qm_ref[0, h], _NT, preferred_element_type=F32)
              for h in range(N_MEM_HEADS)]
    outs = []
    for h in range(N_MEM_HEADS):
        s = scores[h].astype(BF16)
        p = jnp.exp2(s - jnp.max(s, axis=0, keepdims=True))
        vt = vmt_ref[0, h * VROWS:(h + 1) * VROWS, :]
        outs.append(_normalised(jnp.dot(vt, p, preferred_element_type=F32)))
    for h in range(0, N_MEM_HEADS, 2):
        _write_pair_t(out_ref, TOK_W + h * HEAD_DIM, outs[h], outs[h + 1])


def _normalised(acc):
    return acc[:HEAD_DIM] * (1.0 / acc[HEAD_DIM:HEAD_DIM + 1])


def _attend_init(n_states, m_s, acc_s):
    neg = jnp.full((1, TQ), NEG, BF16).astype(F32)
    for st in range(n_states):
        m_s[st] = neg
        acc_s[st] = jnp.zeros((VROWS, TQ), F32)


def _attend_chunks(groups, q_ref, lo, hi, m_s, acc_s):
    def body(kc, _):
        scores = []
        for (h0, nh, _, k_chunk, _, _) in groups:
            q_g = q_ref[0, h0:h0 + nh].reshape(nh * TQ, HEAD_DIM)
            scores.append(lax.dot_general(k_chunk(kc), q_g, _NT, preferred_element_type=F32))
        for (h0, nh, st0, _, vt_chunk, bias_chunk), s_g in zip(groups, scores):
            for j in range(nh):
                st = st0 + j
                s = s_g[:, j * TQ:(j + 1) * TQ].astype(BF16) + bias_chunk(kc)
                m = m_s[st]
                m_new = jnp.maximum(m, jnp.max(s, axis=0, keepdims=True).astype(F32))
                m_s[st] = m_new
                p = jnp.exp2(s - m_new.astype(BF16))
                pv = jnp.dot(vt_chunk(kc), p, preferred_element_type=F32)
                acc_s[st] = jnp.exp2(m - m_new) * acc_s[st] + pv
        return 0

    lax.fori_loop(lo, hi, body, 0)


def _dsa_attn_kernel(q_ref, qi_ref, wt_ref, kidx_ref, k_ref, vt_ref, qm_ref, km_ref, vmt_ref,
                     out_ref, key_s, top_s, bias_s, m_s, o_s, *, topk):
    i = pl.program_id(1)
    nk = ((i + 1) * TQ + TK - 1) // TK
    t_pos = i * TQ + lax.broadcasted_iota(jnp.int32, (TK, TQ), 1)
    row = lax.broadcasted_iota(jnp.int32, (TK, TQ), 0)
    wt = wt_ref[0]

    def rows(kc):
        return pl.ds(pl.multiple_of(kc * TK, TK), TK)

    def score_body(kc, _):
        kx = kidx_ref[0, rows(kc), :]
        sc = jnp.zeros((TK, TQ), F32)
        for h in range(IDX_HEADS):
            lg = lax.dot_general(kx, qi_ref[0, h], _NT, preferred_element_type=F32)
            sc = sc + jnp.maximum(lg, 0.0) * wt[h:h + 1, :]
        sc = jnp.where(sc == 0.0, 0.0, sc)
        bits = pltpu.bitcast(sc, jnp.int32)
        key = bits ^ ((bits >> 31) & 0x7FFFFFFF)
        valid = kc * TK + row <= t_pos
        key_s[rows(kc), :] = jnp.where(valid, key, INT_MIN)
        top = pltpu.bitcast(bits & -65536, F32)
        top_s[rows(kc), :] = jnp.where(valid, top, -jnp.inf).astype(BF16)
        return 0

    lax.fori_loop(0, nk, score_body, 0)

    n_acc = 4 * 8

    def count(pred_fn):
        def body(kc, c):
            hit = jnp.where(pred_fn(key_s[rows(kc), :]), 1.0, 0.0)
            return c + jnp.sum(hit.reshape(TK // n_acc, n_acc, TQ), axis=0)

        c = lax.fori_loop(0, nk, body, jnp.zeros((n_acc, TQ), F32))
        return jnp.sum(c, axis=0, keepdims=True)

    def count_top(cand):
        one, zero = jnp.ones((), BF16), jnp.zeros((), BF16)

        def body(kc, c):
            hit = jnp.where(top_s[rows(kc), :] >= cand, one, zero)
            parts = [hit[r * n_acc:(r + 1) * n_acc] for r in range(TK // n_acc)]
            while len(parts) > 1:
                parts = [a + b for a, b in zip(parts[0::2], parts[1::2])]
            return c + parts[0].astype(F32)

        c = lax.fori_loop(0, nk, body, jnp.zeros((n_acc, TQ), F32))
        return jnp.sum(c, axis=0, keepdims=True)

    def top_bit_body(b, carry):
        ans, n_ge = carry
        cand = ans + jnp.left_shift(jnp.int32(1), 15 - b)
        cand_bits = jnp.left_shift(cand ^ ((cand >> 31) & 0x7FFF), 16)
        cnt = count_top(pltpu.bitcast(cand_bits, F32).astype(BF16))
        take = cnt >= topk
        return jnp.where(take, cand, ans), jnp.where(take, cnt, n_ge)

    def bit_body(b, carry):
        ans, n_ge = carry
        cand = ans + jnp.left_shift(jnp.int32(1), 31 - b)
        cnt = count(lambda keys: keys >= cand)
        take = cnt >= topk
        return jnp.where(take, cand, ans), jnp.where(take, cnt, n_ge)

    top, n_ge = lax.fori_loop(
        0, 16, top_bit_body,
        (jnp.full((1, TQ), -2 ** 15, jnp.int32), jnp.full((1, TQ), float(topk), F32)))
    thr, n_ge = lax.fori_loop(16, 32, bit_body, (jnp.left_shift(top, 16), n_ge))
    has_ties = jnp.max(jnp.where((n_ge > topk) & (thr > INT_MIN), 1.0, 0.0)) > 0.0

    @pl.when(jnp.logical_not(has_ties))
    def _():
        def bias_body(kc, _):
            sel = (key_s[rows(kc), :] >= thr) & (kc * TK + row <= t_pos)
            bias_s[rows(kc), :] = jnp.where(sel, 0.0, NEG).astype(BF16)
            return 0

        lax.fori_loop(0, nk, bias_body, 0)

    @pl.when(has_ties)
    def _():
        need = topk - count(lambda keys: keys > thr)
        tri = (lax.broadcasted_iota(jnp.int32, (TK, TK), 1)
               < lax.broadcasted_iota(jnp.int32, (TK, TK), 0)).astype(BF16)

        def bias_body(kc, eq_seen):
            keys = key_s[rows(kc), :]
            eq = keys == thr
            eq_f = jnp.where(eq, 1.0, 0.0)
            before = eq_seen + jnp.dot(tri, eq_f.astype(BF16), preferred_element_type=F32)
            sel = (keys > thr) | (eq & (before < need))
            sel = sel & (kc * TK + row <= t_pos)
            bias_s[rows(kc), :] = jnp.where(sel, 0.0, NEG).astype(BF16)
            return eq_seen + jnp.sum(eq_f, axis=0, keepdims=True)

        lax.fori_loop(0, nk, bias_body, jnp.zeros((1, TQ), F32))

    _attend_init(N_TOK_HEADS, m_s, o_s)
    _attend_chunks(
        [(0, N_TOK_HEADS, 0,
          lambda kc: k_ref[0, rows(kc), :],
          lambda kc: vt_ref[0, kc],
          lambda kc: bias_s[rows(kc), :])],
        q_ref, 0, nk, m_s, o_s)
    for h in range(0, N_TOK_HEADS, 2):
        _write_pair_t(out_ref, h * HEAD_DIM, _normalised(o_s[h]), _normalised(o_s[h + 1]))
    _mem_attention_t(qm_ref, km_ref, vmt_ref, out_ref)


def _dsa_attn(q, qi, wt, kidx, k, vt, qm, km, vmt, B, S):
    NM = km.shape[2]
    topk = min(DSA_TOPK_MAX, S // 4)
    assert S % TK == 0
    key_rows = S
    stat = pltpu.VMEM((N_TOK_HEADS, 1, TQ), F32)
    qblk = lambda nh: pl.BlockSpec((1, nh, TQ, HEAD_DIM), lambda b, i: (b, 0, i, 0))
    full = pl.BlockSpec((1, S, HEAD_DIM), lambda b, i: (b, 0, 0))
    return pl.pallas_call(
        functools.partial(_dsa_attn_kernel, topk=topk),
        grid=(B, S // TQ),
        in_specs=[
            qblk(N_TOK_HEADS), qblk(IDX_HEADS),
            pl.BlockSpec((1, IDX_HEADS, TQ), lambda b, i: (b, 0, i)),
            full, full,
            pl.BlockSpec((1, S // TK, VROWS, TK), lambda b, i: (b, 0, 0, 0)),
            qblk(N_MEM_HEADS),
            pl.BlockSpec((1, N_MEM_HEADS, NM, HEAD_DIM), lambda b, i: (b, 0, 0, 0)),
            pl.BlockSpec((1, N_MEM_HEADS * VROWS, NM), lambda b, i: (b, 0, 0)),
        ],
        out_specs=pl.BlockSpec((1, TQ, TOK_W + MEM_Q), lambda b, i: (b, i, 0)),
        out_shape=jax.ShapeDtypeStruct((B, S, TOK_W + MEM_Q), BF16),
        scratch_shapes=[
            pltpu.VMEM((key_rows, TQ), jnp.int32),
            pltpu.VMEM((key_rows, TQ), BF16),
            pltpu.VMEM((S, TQ), BF16),
            stat,
            pltpu.VMEM((N_TOK_HEADS, VROWS, TQ), F32),
        ],
        compiler_params=_cparams(2),
        name="dsa_attn",
    )(q, qi, wt, kidx, k, vt, qm, km, vmt)


def _nsa_compress_kernel(kc_ref, vc_ref, pk_ref, pv_ref, kw1_ref, kw2_ref, vw1_ref, vw2_ref,
                         ko_ref, vo_ref):
    half = (CMP_LEN // 2) * HEAD_DIM

    def run(x_ref, pos_ref, w1_ref, w2_ref, o_ref):
        n_rows = x_ref.shape[2] // CMP_STRIDE
        r = jnp.concatenate(
            [x_ref[0, 0, pl.ds(l, n_rows, stride=CMP_STRIDE), :] for l in range(CMP_STRIDE)], axis=1)
        nxt = pltpu.roll(r, r.shape[0] - 1, 0)
        a = (r + pos_ref[0:1, :]).astype(BF16)
        b = (nxt + pos_ref[1:2, :]).astype(BF16)
        hid = (jnp.dot(a, w1_ref[:half, :], preferred_element_type=F32)
               + jnp.dot(b, w1_ref[half:, :], preferred_element_type=F32))
        hid = jax.nn.gelu(hid).astype(BF16)
        o_ref[0, 0] = jnp.dot(hid, w2_ref[...], preferred_element_type=F32).astype(o_ref.dtype)

    run(kc_ref, pk_ref, kw1_ref, kw2_ref, ko_ref)
    run(vc_ref, pv_ref, vw1_ref, vw2_ref, vo_ref)


def _nsa_compress(kc, vc, pos_k, pos_v, k_w1, k_w2, v_w1, v_w2, B, S):
    G = NSA_GROUPS
    R = S // CMP_STRIDE
    W = CMP_STRIDE * HEAD_DIM
    xs = pl.BlockSpec((1, 1, S, HEAD_DIM), lambda b, g: (b, g, 0, 0))
    cst = lambda shp: pl.BlockSpec(shp, lambda b, g: (0, 0))
    osz = pl.BlockSpec((1, 1, R, HEAD_DIM), lambda b, g: (b, g, 0, 0))
    osh = jax.ShapeDtypeStruct((B, G, R, HEAD_DIM), BF16)
    return pl.pallas_call(
        _nsa_compress_kernel,
        grid=(B, G),
        in_specs=[xs, xs, cst((2, W)), cst((2, W)),
                  cst((CMP_LEN * HEAD_DIM, CMP_HIDDEN)), cst((CMP_HIDDEN, HEAD_DIM)),
                  cst((CMP_LEN * HEAD_DIM, CMP_HIDDEN)), cst((CMP_HIDDEN, HEAD_DIM))],
        out_specs=[osz, osz],
        out_shape=[osh, osh],
        compiler_params=_cparams(2),
        name="nsa_compress",
    )(kc, vc, pos_k.reshape(2, W), pos_v.reshape(2, W),
      k_w1.astype(BF16), k_w2.astype(BF16), v_w1.astype(BF16), v_w2.astype(BF16))


def _nsa_attn_kernel(qraw_ref, qrot_ref, kcmp_ref, vcmp_ref, ks_ref, vst_ref, kw_ref, vwt_ref,
                     gt_ref, ovt_ref, qm_ref, km_ref, vmt_ref, out_ref,
                     sel_s, bsel_s, wb_s, m_s, oc_s, osw_s, *, n_cmp, n_slc, n_sel):
    i = pl.program_id(1)
    J = NSA_HPG
    L = J * TQ
    NC = kcmp_ref.shape[2]
    NSP = sel_s.shape[1]
    t_row = i * TQ + lax.broadcasted_iota(jnp.int32, (1, TQ), 1)
    t_pos = i * TQ + lax.broadcasted_iota(jnp.int32, (TK, TQ), 1)
    row = lax.broadcasted_iota(jnp.int32, (TK, TQ), 0)
    gt = gt_ref[0]

    c_idx = lax.broadcasted_iota(jnp.int32, (NC, TQ), 0)
    t_c = i * TQ + lax.broadcasted_iota(jnp.int32, (NC, TQ), 1)
    mask_c1 = (c_idx * CMP_STRIDE + CMP_LEN - 1 <= t_c) & (c_idx < n_cmp)
    mask_c = jnp.concatenate([mask_c1] * J, axis=1)

    n_idx = lax.broadcasted_iota(jnp.int32, (NSP, TQ), 0)
    cur = t_row // SLC_LEN
    forced = (n_idx == 0) | (n_idx == cur) | (n_idx == cur - 1)
    admissible = (n_idx <= cur) & (n_idx < n_slc)

    for g in range(NSA_GROUPS):
        h0 = g * J
        q_raw = qraw_ref[0, h0:h0 + J].reshape(L, HEAD_DIM)
        s_c = lax.dot_general(kcmp_ref[0, g], q_raw, _NT, preferred_element_type=F32)
        s_c = jnp.where(mask_c, s_c, -jnp.inf)
        m_c = jnp.max(s_c, axis=0, keepdims=True)
        m_c = jnp.where(m_c > -jnp.inf, m_c, 0.0)
        p_c = jnp.exp2(s_c - m_c)
        p_c = p_c * (1.0 / jnp.maximum(jnp.sum(p_c, axis=0, keepdims=True), 1e-30))
        o_c = lax.dot_general(vcmp_ref[0, g], p_c.astype(BF16), _TN, preferred_element_type=F32)

        p_sum = p_c[:, 0:TQ]
        for j in range(1, J):
            p_sum = p_sum + p_c[:, j * TQ:(j + 1) * TQ]
        imp = jnp.dot(ovt_ref[...], p_sum, preferred_element_type=F32,
                      precision=lax.Precision.HIGHEST)
        imp = jnp.where(forced, FORCE_SCORE, imp)
        imp = jnp.where(admissible, imp, -jnp.inf)
        rank = jnp.zeros((NSP, TQ), F32)
        for mrow in range(n_slc):
            other = imp[mrow:mrow + 1, :]
            ahead = (other > imp) | ((other == imp) & (n_idx > mrow))
            rank = rank + jnp.where(ahead, 1.0, 0.0)
        sel_s[g] = jnp.where(rank < n_sel, 1.0, 0.0)
        for j in range(J):
            oc_s[h0 + j] = o_c[:, j * TQ:(j + 1) * TQ]

    def rows(kc):
        return pl.ds(pl.multiple_of(kc * TK, TK), TK)

    tiles_per_chunk = TK // TQ
    far = WIN // TQ + 1
    rq = lax.broadcasted_iota(jnp.int32, (TQ, TQ), 0)
    cq = lax.broadcasted_iota(jnp.int32, (TQ, TQ), 1)
    for t in range(far + tiles_per_chunk):
        d = far - t
        ok = (rq <= cq + d * TQ) & (rq > cq - WIN + d * TQ)
        wb_s[t * TQ:(t + 1) * TQ, :] = jnp.where(ok, 0.0, NEG).astype(BF16)

    def win_bias(kc):
        t0 = far - i + kc * tiles_per_chunk
        return wb_s[pl.ds(pl.multiple_of(t0 * TQ, TQ), TK), :]

    nk = ((i + 1) * TQ + TK - 1) // TK
    lo = (jnp.maximum(i - WIN // TQ, 0) * TQ) // TK
    per = TK // SLC_LEN
    for g in range(NSA_GROUPS):
        def bias_body(kc, _, g=g):
            picked = jnp.concatenate(
                [jnp.broadcast_to(sel_s[g, pl.ds(kc * per + r, 1), :], (SLC_LEN, TQ))
                 for r in range(per)], axis=0) > 0.5
            ok = picked & (kc * TK + row <= t_pos)
            bsel_s[g, rows(kc), :] = jnp.where(ok, 0.0, NEG).astype(BF16)
            return 0

        lax.fori_loop(0, nk, bias_body, 0)

    def group_spec(g, st0, k_ref_, vt_ref_, bias_chunk):
        return (g * J, J, st0 + g * J,
                lambda kc: k_ref_[0, g, rows(kc), :],
                lambda kc: vt_ref_[0, kc, g * VROWS:(g + 1) * VROWS, :],
                bias_chunk)

    sel_groups = [group_spec(g, 0, ks_ref, vst_ref, lambda kc, g=g: bsel_s[g, rows(kc), :])
                  for g in range(NSA_GROUPS)]
    win_groups = [group_spec(g, N_TOK_HEADS, kw_ref, vwt_ref, win_bias) for g in range(NSA_GROUPS)]
    _attend_init(2 * N_TOK_HEADS, m_s, osw_s)
    _attend_chunks(sel_groups, qrot_ref, 0, lo, m_s, osw_s)
    _attend_chunks(sel_groups + win_groups, qrot_ref, lo, nk, m_s, osw_s)

    def gated(h):
        return (gt[3 * h:3 * h + 1, :] * oc_s[h]
                + gt[3 * h + 1:3 * h + 2, :] * _normalised(osw_s[h])
                + gt[3 * h + 2:3 * h + 3, :] * _normalised(osw_s[N_TOK_HEADS + h]))

    for h in range(0, N_TOK_HEADS, 2):
        _write_pair_t(out_ref, h * HEAD_DIM, gated(h), gated(h + 1))
    _mem_attention_t(qm_ref, km_ref, vmt_ref, out_ref)


def _nsa_attn(qraw, qrot, kcmp, vcmp, ks, vst, kw, vwt, gt, qm, km, vmt, B, S):
    G = NSA_GROUPS
    NM = km.shape[2]
    NC = kcmp.shape[2]
    n_cmp = (S - CMP_LEN) // CMP_STRIDE + 1
    n_slc = S // SLC_LEN
    n_sel = min(SLC_TOP_MAX, n_slc)
    nsp = -(-n_slc // 8) * 8
    c0 = np.arange(NC) * CMP_STRIDE
    s0 = np.arange(nsp) * SLC_LEN
    ov = np.minimum(c0[None, :] + CMP_LEN, s0[:, None] + SLC_LEN) - np.maximum(c0[None, :], s0[:, None])
    ovt = (np.clip(ov, 0, None) / CMP_LEN).astype(np.float32)
    ovt[:, n_cmp:] = 0.0
    ovt[n_slc:, :] = 0.0

    qblk = lambda nh: pl.BlockSpec((1, nh, TQ, HEAD_DIM), lambda b, i: (b, 0, i, 0))
    full = pl.BlockSpec((1, G, S, HEAD_DIM), lambda b, i: (b, 0, 0, 0))
    full_t = pl.BlockSpec((1, S // TK, G * VROWS, TK), lambda b, i: (b, 0, 0, 0))
    cmpspec = pl.BlockSpec((1, G, NC, HEAD_DIM), lambda b, i: (b, 0, 0, 0))
    head_out = pltpu.VMEM((N_TOK_HEADS, HEAD_DIM, TQ), F32)
    head_acc = pltpu.VMEM((2 * N_TOK_HEADS, VROWS, TQ), F32)
    stat = pltpu.VMEM((2 * N_TOK_HEADS, 1, TQ), F32)
    return pl.pallas_call(
        functools.partial(_nsa_attn_kernel, n_cmp=n_cmp, n_slc=n_slc, n_sel=n_sel),
        grid=(B, S // TQ),
        in_specs=[
            qblk(N_TOK_HEADS), qblk(N_TOK_HEADS), cmpspec, cmpspec, full, full_t, full, full_t,
            pl.BlockSpec((1, GATE_ROWS, TQ), lambda b, i: (b, 0, i)),
            pl.BlockSpec((nsp, NC), lambda b, i: (0, 0)),
            qblk(N_MEM_HEADS),
            pl.BlockSpec((1, N_MEM_HEADS, NM, HEAD_DIM), lambda b, i: (b, 0, 0, 0)),
            pl.BlockSpec((1, N_MEM_HEADS * VROWS, NM), lambda b, i: (b, 0, 0)),
        ],
        out_specs=pl.BlockSpec((1, TQ, TOK_W + MEM_Q), lambda b, i: (b, i, 0)),
        out_shape=jax.ShapeDtypeStruct((B, S, TOK_W + MEM_Q), BF16),
        scratch_shapes=[
            pltpu.VMEM((G, nsp, TQ), F32),
            pltpu.VMEM((G, S, TQ), BF16),
            pltpu.VMEM(((WIN // TQ + 1 + TK // TQ) * TQ, TQ), BF16),
            stat,
            head_out, head_acc,
        ],
        compiler_params=_cparams(2),
        name="nsa_attn",
    )(qraw, qrot, kcmp, vcmp, ks, vst, kw, vwt, gt, jnp.asarray(ovt), qm, km, vmt)


def _post_attn_kernel(x_ref, mix_ref, wo_ref, g_ref, win_ref, wdn_ref, gf_ref, out_ref, *, final):
    d_ff = wdn_ref.shape[0]
    x1 = x_ref[...] + jnp.dot(mix_ref[...], wo_ref[...], preferred_element_type=F32)
    h = _rms(x1, g_ref[...]).astype(BF16)
    gate = jnp.dot(h, win_ref[:, :d_ff], preferred_element_type=F32)
    up = jnp.dot(h, win_ref[:, d_ff:], preferred_element_type=F32)
    act = (jax.nn.silu(gate) * up).astype(BF16)
    x2 = x1 + jnp.dot(act, wdn_ref[...], preferred_element_type=F32)
    if final:
        x2 = _rms(x2, gf_ref[...])
    out_ref[...] = x2


def _post_attn(x2d, mix2d, w_o, g, w_in, w_down, g_final, final):
    N, D = x2d.shape
    MW = mix2d.shape[1]
    d_ff = w_down.shape[0]
    tm = min(TM_FFN, N)
    const = lambda shp: pl.BlockSpec(shp, lambda i: (0, 0), pipeline_mode=pl.Buffered(1))
    return pl.pallas_call(
        functools.partial(_post_attn_kernel, final=final),
        grid=(N // tm,),
        in_specs=[
            pl.BlockSpec((tm, D), lambda i: (i, 0)),
            pl.BlockSpec((tm, MW), lambda i: (i, 0)),
            const((MW, D)), const((1, D)), const((D, 2 * d_ff)), const((d_ff, D)), const((1, D)),
        ],
        out_specs=pl.BlockSpec((tm, D), lambda i: (i, 0)),
        out_shape=jax.ShapeDtypeStruct((N, D), F32),
        compiler_params=_cparams(1),
        name="post_attn",
    )(x2d, mix2d, w_o.astype(BF16), g.reshape(1, D), w_in.astype(BF16), w_down.astype(BF16),
      g_final.reshape(1, D))


def kernel(x, mem, attn_norm, mem_norm, ffn_norm, final_norm, dsa_w_in, dsa_ckv_norm, dsa_w_uk, dsa_w_uv, nsa_w_in, nsa_cmp_pos_k, nsa_cmp_pos_v, nsa_cmp_k_w1, nsa_cmp_k_w2, nsa_cmp_v_w1, nsa_cmp_v_w2, mem_w_kv, w_o, ffn_w_in, ffn_w_down):
    B, S, D = x.shape
    depth = attn_norm.shape[0]
    assert S % TM_PROJ == 0 or S < TM_PROJ
    assert S % TQ == 0 and (B * S) % TM_FFN == 0
    tabs = _rope_tables(S)
    x2d = x.reshape(B * S, D)
    for i in range(depth):
        km, vm = _mem_kv(mem, mem_norm[i], mem_w_kv[i])
        if i % 2 == 0:
            a = i // 2
            q, qi, kidx, k, v, wt, qm = _proj_dsa(
                x2d, attn_norm[i], dsa_w_in[a], dsa_ckv_norm[a], dsa_w_uk[a], dsa_w_uv[a], tabs, B, S)
            mix = _dsa_attn(q, qi, wt, kidx, k, v, qm, km, vm, B, S)
        else:
            b = i // 2
            qraw, qrot, kc, vc, ks, vs, kw, vw, gt, qm = _proj_nsa(
                x2d, attn_norm[i], nsa_w_in[b], tabs, B, S)
            kcmp, vcmp = _nsa_compress(
                kc, vc, nsa_cmp_pos_k[b], nsa_cmp_pos_v[b],
                nsa_cmp_k_w1[b], nsa_cmp_k_w2[b], nsa_cmp_v_w1[b], nsa_cmp_v_w2[b], B, S)
            mix = _nsa_attn(qraw, qrot, kcmp, vcmp, ks, vs, kw, vw, gt, qm, km, vm, B, S)
        x2d = _post_attn(x2d, mix.reshape(B * S, TOK_W + MEM_Q), w_o[i], ffn_norm[i],
                         ffn_w_in[i], ffn_w_down[i], final_norm, final=(i == depth - 1))
    return x2d.reshape(B, S, D)
```

```python
import functools
import math

import numpy as np
import jax
import jax.numpy as jnp
from jax import lax
from jax.experimental import pallas as pl
from jax.experimental.pallas import tpu as pltpu

F32 = jnp.float32
BF16 = jnp.bfloat16

HEAD_DIM = 64
ROPE_DIM = 16
ROPE_THETA = 500000.0
RMS_EPS = 1e-6
N_TOK_HEADS = 12
N_MEM_HEADS = 4
TOK_W = N_TOK_HEADS * HEAD_DIM
MEM_Q = N_MEM_HEADS * HEAD_DIM
DSA_KV_RANK = 128
DSA_NOPE = HEAD_DIM - ROPE_DIM
IDX_HEADS = 8
IDX_DIM = 64
DSA_TOPK_MAX = 256
NSA_GROUPS = 2
NSA_HPG = N_TOK_HEADS // NSA_GROUPS
CMP_LEN = 32
CMP_STRIDE = 16
CMP_HIDDEN = 128
SLC_LEN = 64
SLC_TOP_MAX = 16
WIN = 512
FORCE_SCORE = 1e9
N_GATES = N_TOK_HEADS * 3
GATE_ROWS = 40

LANE = 128
TQ = 256
TK = 256
ONES_ROWS = 16
VROWS = HEAD_DIM + ONES_ROWS
TM_PROJ = 512
TM_FFN = 512
VMEM_LIMIT = 56 * 1024 * 1024
NEG = -1e30
SCALE_LOG2 = HEAD_DIM ** -0.5 * math.log2(math.e)
INT_MIN = -2 ** 31

_NT = (((1,), (1,)), ((), ()))
_TN = (((0,), (0,)), ((), ()))


def _cparams(n_axes):
    return pltpu.CompilerParams(
        dimension_semantics=("arbitrary",) * n_axes, vmem_limit_bytes=VMEM_LIMIT)


def _rms(xf, g):
    ms = jnp.mean(xf * xf, axis=-1, keepdims=True)
    return xf * lax.rsqrt(ms + RMS_EPS) * g


def _rope_tables(S):
    inv = ROPE_THETA ** (-np.arange(0, ROPE_DIM, 2, dtype=np.float64) / ROPE_DIM)
    ang = np.arange(S, dtype=np.float64)[:, None] * inv[None, :]
    cos = np.cos(ang).astype(np.float32)
    sin = np.sin(ang).astype(np.float32)
    half = ROPE_DIM // 2
    c = np.ones((S, LANE), np.float32)
    s_up = np.zeros((S, LANE), np.float32)
    s_dn = np.zeros((S, LANE), np.float32)
    for base in range(0, LANE, HEAD_DIM):
        c[:, base:base + half] = cos
        c[:, base + half:base + ROPE_DIM] = cos
        s_up[:, base + half:base + ROPE_DIM] = sin
        s_dn[:, base:base + half] = -sin
    return jnp.asarray(c), jnp.asarray(s_up), jnp.asarray(s_dn)


def _rope(x, c, s_up, s_dn):
    half = ROPE_DIM // 2
    outs = []
    for j in range(x.shape[1] // LANE):
        xc = x[:, j * LANE:(j + 1) * LANE]
        outs.append(xc * c + pltpu.roll(xc, half, 1) * s_up + pltpu.roll(xc, LANE - half, 1) * s_dn)
    return outs


def _store_heads(ref, chunks, dtype):
    for j, ch in enumerate(chunks):
        ref[0, 2 * j, :, :] = ch[:, :HEAD_DIM].astype(dtype)
        ref[0, 2 * j + 1, :, :] = ch[:, HEAD_DIM:].astype(dtype)


def _chunks(x):
    return [x[:, j * LANE:(j + 1) * LANE] for j in range(x.shape[1] // LANE)]


def _store_values_t(ref, x, row_lo, n_groups):
    xt = x.T
    ones = jnp.ones((ONES_ROWS, TK), ref.dtype)
    for j in range(x.shape[0] // TK):
        for g in range(n_groups):
            r0 = row_lo + g * HEAD_DIM
            ref[0, j, g * VROWS:g * VROWS + HEAD_DIM, :] = (
                xt[r0:r0 + HEAD_DIM, j * TK:(j + 1) * TK].astype(ref.dtype))
            ref[0, j, g * VROWS + HEAD_DIM:(g + 1) * VROWS, :] = ones


def _proj_dsa_kernel(x_ref, g_ref, w_ref, gkv_ref, wkv_ref, c_ref, su_ref, sd_ref,
                     q_ref, qi_ref, kidx_ref, k_ref, vt_ref, wt_ref, qm_ref):
    h = _rms(x_ref[...], g_ref[...]).astype(BF16)
    c, su, sd = c_ref[...], su_ref[...], sd_ref[...]

    def mm(lo, hi):
        return jnp.dot(h, w_ref[:, lo:hi], preferred_element_type=F32)

    q = _rope(mm(0, TOK_W), c, su, sd)
    _store_heads(q_ref, [ch * SCALE_LOG2 for ch in q], BF16)

    ckv = _rms(mm(TOK_W, TOK_W + DSA_KV_RANK), gkv_ref[...]).astype(BF16)
    kvn = jnp.dot(ckv, wkv_ref[...], preferred_element_type=F32)

    o_qi = TOK_W + DSA_KV_RANK
    qi = _rope(mm(o_qi, o_qi + IDX_HEADS * IDX_DIM), c, su, sd)
    _store_heads(qi_ref, qi, BF16)

    o_misc = o_qi + IDX_HEADS * IDX_DIM
    misc = _rope(mm(o_misc, o_misc + LANE), c, su, sd)[0]
    kidx_ref[0] = misc[:, :IDX_DIM].astype(BF16)
    lane = lax.broadcasted_iota(jnp.int32, misc.shape, 1)
    k_full = jnp.where(lane < ROPE_DIM, pltpu.roll(misc, HEAD_DIM, 1), kvn)
    k_ref[0] = k_full[:, :HEAD_DIM].astype(BF16)
    _store_values_t(vt_ref, kvn, HEAD_DIM, 1)
    w_scaled = misc * (IDX_HEADS ** -0.5 * IDX_DIM ** -0.5)
    w_row0 = IDX_DIM + ROPE_DIM
    wt_ref[0] = w_scaled.T[w_row0:w_row0 + IDX_HEADS, :]

    o_qm = o_misc + LANE
    qm = mm(o_qm, o_qm + MEM_Q)
    _store_heads(qm_ref, [ch * SCALE_LOG2 for ch in _chunks(qm)], BF16)


def _proj_dsa(x2d, g, w_in, gkv, w_uk, w_uv, tabs, B, S):
    D = x2d.shape[1]
    o = np.cumsum([0, TOK_W, DSA_KV_RANK, ROPE_DIM, IDX_HEADS * IDX_DIM, IDX_DIM, IDX_HEADS, MEM_Q])
    q, ckv, kr, qi, ki, wi, qm = [w_in[:, o[i]:o[i + 1]] for i in range(7)]
    pad = jnp.zeros((D, LANE - IDX_DIM - ROPE_DIM - IDX_HEADS), w_in.dtype)
    w = jnp.concatenate([q, ckv, qi, ki, kr, wi, pad, qm], axis=1).astype(BF16)
    wkv = jnp.concatenate([jnp.zeros((DSA_KV_RANK, ROPE_DIM), F32), w_uk, w_uv], axis=1).astype(BF16)
    ncol = w.shape[1]
    tm = min(TM_PROJ, S)
    nst = S // tm
    tok = lambda i: (i, 0)
    const = lambda i: (0, 0)
    tab = lambda i: (i % nst, 0)
    hm = lambda i: (i // nst, 0, i % nst, 0)
    row = lambda i: (i // nst, i % nst, 0)
    outs = pl.pallas_call(
        _proj_dsa_kernel,
        grid=(B * nst,),
        in_specs=[
            pl.BlockSpec((tm, D), tok),
            pl.BlockSpec((1, D), const),
            pl.BlockSpec((D, ncol), const),
            pl.BlockSpec((1, DSA_KV_RANK), const),
            pl.BlockSpec((DSA_KV_RANK, LANE), const),
            pl.BlockSpec((tm, LANE), tab),
            pl.BlockSpec((tm, LANE), tab),
            pl.BlockSpec((tm, LANE), tab),
        ],
        out_specs=[
            pl.BlockSpec((1, N_TOK_HEADS, tm, HEAD_DIM), hm),
            pl.BlockSpec((1, IDX_HEADS, tm, IDX_DIM), hm),
            pl.BlockSpec((1, tm, IDX_DIM), row),
            pl.BlockSpec((1, tm, HEAD_DIM), row),
            pl.BlockSpec((1, tm // TK, VROWS, TK), lambda i: (i // nst, i % nst, 0, 0)),
            pl.BlockSpec((1, IDX_HEADS, tm), lambda i: (i // nst, 0, i % nst)),
            pl.BlockSpec((1, N_MEM_HEADS, tm, HEAD_DIM), hm),
        ],
        out_shape=[
            jax.ShapeDtypeStruct((B, N_TOK_HEADS, S, HEAD_DIM), BF16),
            jax.ShapeDtypeStruct((B, IDX_HEADS, S, IDX_DIM), BF16),
            jax.ShapeDtypeStruct((B, S, IDX_DIM), BF16),
            jax.ShapeDtypeStruct((B, S, HEAD_DIM), BF16),
            jax.ShapeDtypeStruct((B, S // TK, VROWS, TK), BF16),
            jax.ShapeDtypeStruct((B, IDX_HEADS, S), F32),
            jax.ShapeDtypeStruct((B, N_MEM_HEADS, S, HEAD_DIM), BF16),
        ],
        compiler_params=_cparams(1),
        name="proj_dsa",
    )(x2d, g.reshape(1, D), w, gkv.reshape(1, DSA_KV_RANK), wkv, *tabs)
    return outs


def _proj_nsa_kernel(x_ref, g_ref, w_ref, c_ref, su_ref, sd_ref,
                     qraw_ref, qrot_ref, kc_ref, vc_ref, ks_ref, vs_ref, kw_ref, vw_ref,
                     gt_ref, qm_ref):
    h = _rms(x_ref[...], g_ref[...]).astype(BF16)
    c, su, sd = c_ref[...], su_ref[...], sd_ref[...]

    def mm(lo, hi):
        return jnp.dot(h, w_ref[:, lo:hi], preferred_element_type=F32)

    q = mm(0, TOK_W)
    _store_heads(qraw_ref, [ch * SCALE_LOG2 for ch in _chunks(q)], BF16)
    _store_heads(qrot_ref, [ch * SCALE_LOG2 for ch in _rope(q, c, su, sd)], BF16)

    o = TOK_W
    _store_heads(kc_ref, _chunks(mm(o, o + LANE)), F32)
    _store_heads(vc_ref, _chunks(mm(o + LANE, o + 2 * LANE)), F32)
    _store_heads(ks_ref, _rope(mm(o + 2 * LANE, o + 3 * LANE), c, su, sd), BF16)
    _store_values_t(vs_ref, mm(o + 3 * LANE, o + 4 * LANE), 0, NSA_GROUPS)
    _store_heads(kw_ref, _rope(mm(o + 4 * LANE, o + 5 * LANE), c, su, sd), BF16)
    _store_values_t(vw_ref, mm(o + 5 * LANE, o + 6 * LANE), 0, NSA_GROUPS)

    gates = jax.nn.sigmoid(mm(o + 6 * LANE, o + 7 * LANE))
    gt_ref[0] = gates.T[:GATE_ROWS, :]

    qm = mm(o + 7 * LANE, o + 7 * LANE + MEM_Q)
    _store_heads(qm_ref, [ch * SCALE_LOG2 for ch in _chunks(qm)], BF16)


def _proj_nsa(x2d, g, w_in, tabs, B, S):
    D = x2d.shape[1]
    kv_w = NSA_GROUPS * HEAD_DIM
    o_g = TOK_W + 6 * kv_w
    pad = jnp.zeros((D, LANE - N_GATES), w_in.dtype)
    w = jnp.concatenate([w_in[:, :o_g + N_GATES], pad, w_in[:, o_g + N_GATES:]], axis=1).astype(BF16)
    ncol = w.shape[1]
    tm = min(TM_PROJ, S)
    nst = S // tm
    tok = lambda i: (i, 0)
    const = lambda i: (0, 0)
    tab = lambda i: (i % nst, 0)
    hm = lambda i: (i // nst, 0, i % nst, 0)

    def hm_spec(nh):
        return pl.BlockSpec((1, nh, tm, HEAD_DIM), hm)

    def hm_shape(nh, dt):
        return jax.ShapeDtypeStruct((B, nh, S, HEAD_DIM), dt)

    G = NSA_GROUPS
    vt_spec = pl.BlockSpec((1, tm // TK, G * VROWS, TK), lambda i: (i // nst, i % nst, 0, 0))
    vt_shape = jax.ShapeDtypeStruct((B, S // TK, G * VROWS, TK), BF16)
    outs = pl.pallas_call(
        _proj_nsa_kernel,
        grid=(B * nst,),
        in_specs=[
            pl.BlockSpec((tm, D), tok),
            pl.BlockSpec((1, D), const),
            pl.BlockSpec((D, ncol), const),
            pl.BlockSpec((tm, LANE), tab),
            pl.BlockSpec((tm, LANE), tab),
            pl.BlockSpec((tm, LANE), tab),
        ],
        out_specs=[
            hm_spec(N_TOK_HEADS), hm_spec(N_TOK_HEADS),
            hm_spec(G), hm_spec(G), hm_spec(G), vt_spec, hm_spec(G), vt_spec,
            pl.BlockSpec((1, GATE_ROWS, tm), lambda i: (i // nst, 0, i % nst)),
            hm_spec(N_MEM_HEADS),
        ],
        out_shape=[
            hm_shape(N_TOK_HEADS, BF16), hm_shape(N_TOK_HEADS, BF16),
            hm_shape(G, F32), hm_shape(G, F32),
            hm_shape(G, BF16), vt_shape, hm_shape(G, BF16), vt_shape,
            jax.ShapeDtypeStruct((B, GATE_ROWS, S), F32),
            hm_shape(N_MEM_HEADS, BF16),
        ],
        compiler_params=_cparams(1),
        name="proj_nsa",
    )(x2d, g.reshape(1, D), w, *tabs)
    return outs


def _mem_kv_kernel(m_ref, g_ref, w_ref, k_ref, vt_ref):
    h = _rms(m_ref[0], g_ref[...]).astype(BF16)
    kv = jnp.dot(h, w_ref[...], preferred_element_type=F32)
    _store_heads(k_ref, _chunks(kv[:, :MEM_Q]), BF16)
    ones = jnp.ones((ONES_ROWS, kv.shape[0]), BF16)
    for j in range(MEM_Q // LANE):
        pair_t = kv[:, MEM_Q + j * LANE:MEM_Q + (j + 1) * LANE].T.astype(BF16)
        for r in range(LANE // HEAD_DIM):
            h = j * (LANE // HEAD_DIM) + r
            vt_ref[0, h * VROWS:h * VROWS + HEAD_DIM, :] = pair_t[r * HEAD_DIM:(r + 1) * HEAD_DIM, :]
            vt_ref[0, h * VROWS + HEAD_DIM:(h + 1) * VROWS, :] = ones


def _mem_kv(mem, g, w_kv):
    B, NM, D = mem.shape
    assert NM % LANE == 0
    spec = pl.BlockSpec((1, N_MEM_HEADS, NM, HEAD_DIM), lambda b: (b, 0, 0, 0))
    shape = jax.ShapeDtypeStruct((B, N_MEM_HEADS, NM, HEAD_DIM), BF16)
    vt_spec = pl.BlockSpec((1, N_MEM_HEADS * VROWS, NM), lambda b: (b, 0, 0))
    vt_shape = jax.ShapeDtypeStruct((B, N_MEM_HEADS * VROWS, NM), BF16)
    return pl.pallas_call(
        _mem_kv_kernel,
        grid=(B,),
        in_specs=[
            pl.BlockSpec((1, NM, D), lambda b: (b, 0, 0)),
            pl.BlockSpec((1, D), lambda b: (0, 0)),
            pl.BlockSpec((D, 2 * MEM_Q), lambda b: (0, 0)),
        ],
        out_specs=[spec, vt_spec],
        out_shape=[shape, vt_shape],
        compiler_params=_cparams(1),
        name="mem_kv",
    )(mem, g.reshape(1, D), w_kv.astype(BF16))


def _write_pair_t(out_ref, col, o_a, o_b):
    pair = jnp.concatenate([o_a, o_b], axis=0)
    out_ref[0, :, col:col + 2 * HEAD_DIM] = pair.T.astype(out_ref.dtype)


def _mem_attention_t(qm_ref, km_ref, vmt_ref, out_ref):
    scores = [lax.dot_general(km_ref[0, h], qm_ref[0, h], _NT, preferred_element_type=F32)
              for h in range(N_MEM_HEADS)]
    outs = []
    for h in range(N_MEM_HEADS):
        s = scores[h].astype(BF16)
        p = jnp.exp2(s - jnp.max(s, axis=0, keepdims=True))
        vt = vmt_ref[0, h * VROWS:(h + 1) * VROWS, :]
        outs.append(_normalised(jnp.dot(vt, p, preferred_element_type=F32)))
    for h in range(0, N_MEM_HEADS, 2):
        _write_pair_t(out_ref, TOK_W + h * HEAD_DIM, outs[h], outs[h + 1])


def _normalised(acc):
    return acc[:HEAD_DIM] * (1.0 / acc[HEAD_DIM:HEAD_DIM + 1])


def _attend_init(n_states, m_s, acc_s):
    neg = jnp.full((1, TQ), NEG, BF16).astype(F32)
    for st in range(n_states):
        m_s[st] = neg
        acc_s[st] = jnp.zeros((VROWS, TQ), F32)


def _attend_chunks(groups, q_ref, lo, hi, m_s, acc_s):
    def body(kc, _):
        scores = []
        for (h0, nh, _, k_chunk, _, _) in groups:
            q_g = q_ref[0, h0:h0 + nh].reshape(nh * TQ, HEAD_DIM)
            scores.append(lax.dot_general(k_chunk(kc), q_g, _NT, preferred_element_type=F32))
        for (h0, nh, st0, _, vt_chunk, bias_chunk), s_g in zip(groups, scores):
            for j in range(nh):
                st = st0 + j
                s = s_g[:, j * TQ:(j + 1) * TQ].astype(BF16) + bias_chunk(kc)
                m = m_s[st]
                m_new = jnp.maximum(m, jnp.max(s, axis=0, keepdims=True).astype(F32))
                m_s[st] = m_new
                p = jnp.exp2(s - m_new.astype(BF16))
                pv = jnp.dot(vt_chunk(kc), p, preferred_element_type=F32)
                acc_s[st] = jnp.exp2(m - m_new) * acc_s[st] + pv
        return 0

    lax.fori_loop(lo, hi, body, 0)


def _dsa_attn_kernel(q_ref, qi_ref, wt_ref, kidx_ref, k_ref, vt_ref, qm_ref, km_ref, vmt_ref,
                     out_ref, key_s, top_s, bias_s, m_s, o_s, *, topk):
    i = pl.program_id(1)
    nk = ((i + 1) * TQ + TK - 1) // TK
    t_pos = i * TQ + lax.broadcasted_iota(jnp.int32, (TK, TQ), 1)
    row = lax.broadcasted_iota(jnp.int32, (TK, TQ), 0)
    wt = wt_ref[0]

    def rows(kc):
        return pl.ds(pl.multiple_of(kc * TK, TK), TK)

    def score_body(kc, _):
        kx = kidx_ref[0, rows(kc), :]
        sc = jnp.zeros((TK, TQ), F32)
        for h in range(IDX_HEADS):
            lg = lax.dot_general(kx, qi_ref[0, h], _NT, preferred_element_type=F32)
            sc = sc + jnp.maximum(lg, 0.0) * wt[h:h + 1, :]
        sc = jnp.where(sc == 0.0, 0.0, sc)
        bits = pltpu.bitcast(sc, jnp.int32)
        key = bits ^ ((bits >> 31) & 0x7FFFFFFF)
        valid = kc * TK + row <= t_pos
        key_s[rows(kc), :] = jnp.where(valid, key, INT_MIN)
        top = pltpu.bitcast(bits & -65536, F32)
        top_s[rows(kc), :] = jnp.where(valid, top, -jnp.inf).astype(BF16)
        return 0

    lax.fori_loop(0, nk, score_body, 0)

    n_acc = 4 * 8

    def count(pred_fn):
        def body(kc, c):
            hit = jnp.where(pred_fn(key_s[rows(kc), :]), 1.0, 0.0)
            return c + jnp.sum(hit.reshape(TK // n_acc, n_acc, TQ), axis=0)

        c = lax.fori_loop(0, nk, body, jnp.zeros((n_acc, TQ), F32))
        return jnp.sum(c, axis=0, keepdims=True)

    def count_top(cand):
        one, zero = jnp.ones((), BF16), jnp.zeros((), BF16)

        def body(kc, c):
            hit = jnp.where(top_s[rows(kc), :] >= cand, one, zero)
            parts = [hit[r * n_acc:(r + 1) * n_acc] for r in range(TK // n_acc)]
            while len(parts) > 1:
                parts = [a + b for a, b in zip(parts[0::2], parts[1::2])]
            return c + parts[0].astype(F32)

        c = lax.fori_loop(0, nk, body, jnp.zeros((n_acc, TQ), F32))
        return jnp.sum(c, axis=0, keepdims=True)

    def top_bit_body(b, carry):
        ans, n_ge = carry
        cand = ans + jnp.left_shift(jnp.int32(1), 15 - b)
        cand_bits = jnp.left_shift(cand ^ ((cand >> 31) & 0x7FFF), 16)
        cnt = count_top(pltpu.bitcast(cand_bits, F32).astype(BF16))
        take = cnt >= topk
        return jnp.where(take, cand, ans), jnp.where(take, cnt, n_ge)

    def bit_body(b, carry):
        ans, n_ge = carry
        cand = ans + jnp.left_shift(jnp.int32(1), 31 - b)
        cnt = count(lambda keys: keys >= cand)
        take = cnt >= topk
        return jnp.where(take, cand, ans), jnp.where(take, cnt, n_ge)

    top, n_ge = lax.fori_loop(
        0, 16, top_bit_body,
        (jnp.full((1, TQ), -2 ** 15, jnp.int32), jnp.full((1, TQ), float(topk), F32)))
    thr, n_ge = lax.fori_loop(16, 32, bit_body, (jnp.left_shift(top, 16), n_ge))
    has_ties = jnp.max(jnp.where((n_ge > topk) & (thr > INT_MIN), 1.0, 0.0)) > 0.0

    @pl.when(jnp.logical_not(has_ties))
    def _():
        thr_valid = jnp.maximum(thr, INT_MIN + 1)

        def bias_body(kc, _):
            bias_s[rows(kc), :] = jnp.where(key_s[rows(kc), :] >= thr_valid, 0.0, NEG).astype(BF16)
            return 0

        lax.fori_loop(0, nk, bias_body, 0)

    @pl.when(has_ties)
    def _():
        need = topk - count(lambda keys: keys > thr)
        tri = (lax.broadcasted_iota(jnp.int32, (TK, TK), 1)
               < lax.broadcasted_iota(jnp.int32, (TK, TK), 0)).astype(BF16)

        def bias_body(kc, eq_seen):
            keys = key_s[rows(kc), :]
            eq = keys == thr
            eq_f = jnp.where(eq, 1.0, 0.0)
            before = eq_seen + jnp.dot(tri, eq_f.astype(BF16), preferred_element_type=F32)
            sel = (keys > thr) | (eq & (before < need))
            sel = sel & (kc * TK + row <= t_pos)
            bias_s[rows(kc), :] = jnp.where(sel, 0.0, NEG).astype(BF16)
            return eq_seen + jnp.sum(eq_f, axis=0, keepdims=True)

        lax.fori_loop(0, nk, bias_body, jnp.zeros((1, TQ), F32))

    _attend_init(N_TOK_HEADS, m_s, o_s)
    _attend_chunks(
        [(0, N_TOK_HEADS, 0,
          lambda kc: k_ref[0, rows(kc), :],
          lambda kc: vt_ref[0, kc],
          lambda kc: bias_s[rows(kc), :])],
        q_ref, 0, nk, m_s, o_s)
    for h in range(0, N_TOK_HEADS, 2):
        _write_pair_t(out_ref, h * HEAD_DIM, _normalised(o_s[h]), _normalised(o_s[h + 1]))
    _mem_attention_t(qm_ref, km_ref, vmt_ref, out_ref)


def _dsa_attn(q, qi, wt, kidx, k, vt, qm, km, vmt, B, S):
    NM = km.shape[2]
    topk = min(DSA_TOPK_MAX, S // 4)
    assert S % TK == 0
    key_rows = S
    stat = pltpu.VMEM((N_TOK_HEADS, 1, TQ), F32)
    qblk = lambda nh: pl.BlockSpec((1, nh, TQ, HEAD_DIM), lambda b, i: (b, 0, i, 0))
    full = pl.BlockSpec((1, S, HEAD_DIM), lambda b, i: (b, 0, 0))
    return pl.pallas_call(
        functools.partial(_dsa_attn_kernel, topk=topk),
        grid=(B, S // TQ),
        in_specs=[
            qblk(N_TOK_HEADS), qblk(IDX_HEADS),
            pl.BlockSpec((1, IDX_HEADS, TQ), lambda b, i: (b, 0, i)),
            full, full,
            pl.BlockSpec((1, S // TK, VROWS, TK), lambda b, i: (b, 0, 0, 0)),
            qblk(N_MEM_HEADS),
            pl.BlockSpec((1, N_MEM_HEADS, NM, HEAD_DIM), lambda b, i: (b, 0, 0, 0)),
            pl.BlockSpec((1, N_MEM_HEADS * VROWS, NM), lambda b, i: (b, 0, 0)),
        ],
        out_specs=pl.BlockSpec((1, TQ, TOK_W + MEM_Q), lambda b, i: (b, i, 0)),
        out_shape=jax.ShapeDtypeStruct((B, S, TOK_W + MEM_Q), BF16),
        scratch_shapes=[
            pltpu.VMEM((key_rows, TQ), jnp.int32),
            pltpu.VMEM((key_rows, TQ), BF16),
            pltpu.VMEM((S, TQ), BF16),
            stat,
            pltpu.VMEM((N_TOK_HEADS, VROWS, TQ), F32),
        ],
        compiler_params=_cparams(2),
        name="dsa_attn",
    )(q, qi, wt, kidx, k, vt, qm, km, vmt)


def _nsa_compress_kernel(kc_ref, vc_ref, pk_ref, pv_ref, kw1_ref, kw2_ref, vw1_ref, vw2_ref,
                         ko_ref, vo_ref):
    half = (CMP_LEN // 2) * HEAD_DIM

    def run(x_ref, pos_ref, w1_ref, w2_ref, o_ref):
        n_rows = x_ref.shape[2] // CMP_STRIDE
        r = jnp.concatenate(
            [x_ref[0, 0, pl.ds(l, n_rows, stride=CMP_STRIDE), :] for l in range(CMP_STRIDE)], axis=1)
        nxt = pltpu.roll(r, r.shape[0] - 1, 0)
        a = (r + pos_ref[0:1, :]).astype(BF16)
        b = (nxt + pos_ref[1:2, :]).astype(BF16)
        hid = (jnp.dot(a, w1_ref[:half, :], preferred_element_type=F32)
               + jnp.dot(b, w1_ref[half:, :], preferred_element_type=F32))
        hid = jax.nn.gelu(hid).astype(BF16)
        o_ref[0, 0] = jnp.dot(hid, w2_ref[...], preferred_element_type=F32).astype(o_ref.dtype)

    run(kc_ref, pk_ref, kw1_ref, kw2_ref, ko_ref)
    run(vc_ref, pv_ref, vw1_ref, vw2_ref, vo_ref)


def _nsa_compress(kc, vc, pos_k, pos_v, k_w1, k_w2, v_w1, v_w2, B, S):
    G = NSA_GROUPS
    R = S // CMP_STRIDE
    W = CMP_STRIDE * HEAD_DIM
    xs = pl.BlockSpec((1, 1, S, HEAD_DIM), lambda b, g: (b, g, 0, 0))
    cst = lambda shp: pl.BlockSpec(shp, lambda b, g: (0, 0))
    osz = pl.BlockSpec((1, 1, R, HEAD_DIM), lambda b, g: (b, g, 0, 0))
    osh = jax.ShapeDtypeStruct((B, G, R, HEAD_DIM), BF16)
    return pl.pallas_call(
        _nsa_compress_kernel,
        grid=(B, G),
        in_specs=[xs, xs, cst((2, W)), cst((2, W)),
                  cst((CMP_LEN * HEAD_DIM, CMP_HIDDEN)), cst((CMP_HIDDEN, HEAD_DIM)),
                  cst((CMP_LEN * HEAD_DIM, CMP_HIDDEN)), cst((CMP_HIDDEN, HEAD_DIM))],
        out_specs=[osz, osz],
        out_shape=[osh, osh],
        compiler_params=_cparams(2),
        name="nsa_compress",
    )(kc, vc, pos_k.reshape(2, W), pos_v.reshape(2, W),
      k_w1.astype(BF16), k_w2.astype(BF16), v_w1.astype(BF16), v_w2.astype(BF16))


def _nsa_attn_kernel(qraw_ref, qrot_ref, kcmp_ref, vcmp_ref, ks_ref, vst_ref, kw_ref, vwt_ref,
                     gt_ref, ovt_ref, qm_ref, km_ref, vmt_ref, out_ref,
                     sel_s, bsel_s, wb_s, m_s, oc_s, osw_s, *, n_cmp, n_slc, n_sel):
    i = pl.program_id(1)
    J = NSA_HPG
    L = J * TQ
    NC = kcmp_ref.shape[2]
    NSP = sel_s.shape[1]
    t_row = i * TQ + lax.broadcasted_iota(jnp.int32, (1, TQ), 1)
    t_pos = i * TQ + lax.broadcasted_iota(jnp.int32, (TK, TQ), 1)
    row = lax.broadcasted_iota(jnp.int32, (TK, TQ), 0)
    gt = gt_ref[0]

    c_idx = lax.broadcasted_iota(jnp.int32, (NC, TQ), 0)
    t_c = i * TQ + lax.broadcasted_iota(jnp.int32, (NC, TQ), 1)
    mask_c1 = (c_idx * CMP_STRIDE + CMP_LEN - 1 <= t_c) & (c_idx < n_cmp)
    mask_c = jnp.concatenate([mask_c1] * J, axis=1)

    n_idx = lax.broadcasted_iota(jnp.int32, (NSP, TQ), 0)
    cur = t_row // SLC_LEN
    forced = (n_idx == 0) | (n_idx == cur) | (n_idx == cur - 1)
    admissible = (n_idx <= cur) & (n_idx < n_slc)

    for g in range(NSA_GROUPS):
        h0 = g * J
        q_raw = qraw_ref[0, h0:h0 + J].reshape(L, HEAD_DIM)
        s_c = lax.dot_general(kcmp_ref[0, g], q_raw, _NT, preferred_element_type=F32)
        s_c = jnp.where(mask_c, s_c, -jnp.inf)
        m_c = jnp.max(s_c, axis=0, keepdims=True)
        m_c = jnp.where(m_c > -jnp.inf, m_c, 0.0)
        p_c = jnp.exp2(s_c - m_c)
        p_c = p_c * (1.0 / jnp.maximum(jnp.sum(p_c, axis=0, keepdims=True), 1e-30))
        o_c = lax.dot_general(vcmp_ref[0, g], p_c.astype(BF16), _TN, preferred_element_type=F32)

        p_sum = p_c[:, 0:TQ]
        for j in range(1, J):
            p_sum = p_sum + p_c[:, j * TQ:(j + 1) * TQ]
        imp = jnp.dot(ovt_ref[...], p_sum, preferred_element_type=F32,
                      precision=lax.Precision.HIGHEST)
        imp = jnp.where(forced, FORCE_SCORE, imp)
        imp = jnp.where(admissible, imp, -jnp.inf)
        rank = jnp.zeros((NSP, TQ), F32)
        for mrow in range(n_slc):
            other = imp[mrow:mrow + 1, :]
            ahead = (other > imp) | ((other == imp) & (n_idx > mrow))
            rank = rank + jnp.where(ahead, 1.0, 0.0)
        sel_s[g] = jnp.where(rank < n_sel, 1.0, 0.0)
        for j in range(J):
            oc_s[h0 + j] = o_c[:, j * TQ:(j + 1) * TQ]

    def rows(kc):
        return pl.ds(pl.multiple_of(kc * TK, TK), TK)

    tiles_per_chunk = TK // TQ
    far = WIN // TQ + 1
    rq = lax.broadcasted_iota(jnp.int32, (TQ, TQ), 0)
    cq = lax.broadcasted_iota(jnp.int32, (TQ, TQ), 1)
    for t in range(far + tiles_per_chunk):
        d = far - t
        ok = (rq <= cq + d * TQ) & (rq > cq - WIN + d * TQ)
        wb_s[t * TQ:(t + 1) * TQ, :] = jnp.where(ok, 0.0, NEG).astype(BF16)

    def win_bias(kc):
        t0 = far - i + kc * tiles_per_chunk
        return wb_s[pl.ds(pl.multiple_of(t0 * TQ, TQ), TK), :]

    nk = ((i + 1) * TQ + TK - 1) // TK
    lo = (jnp.maximum(i - WIN // TQ, 0) * TQ) // TK
    per = TK // SLC_LEN
    for g in range(NSA_GROUPS):
        def picked_bias(kc, g=g):
            picked = jnp.concatenate(
                [jnp.broadcast_to(sel_s[g, pl.ds(kc * per + r, 1), :], (SLC_LEN, TQ))
                 for r in range(per)], axis=0)
            return jnp.where(picked > 0.5, 0.0, NEG)

        def full_body(kc, _, g=g, picked_bias=picked_bias):
            bsel_s[g, rows(kc), :] = picked_bias(kc).astype(BF16)
            return 0

        def edge_body(kc, _, g=g, picked_bias=picked_bias):
            bias = jnp.where(kc * TK + row <= t_pos, picked_bias(kc), NEG)
            bsel_s[g, rows(kc), :] = bias.astype(BF16)
            return 0

        n_full = (i * TQ + 1) // TK
        lax.fori_loop(0, n_full, full_body, 0)
        lax.fori_loop(n_full, nk, edge_body, 0)

    def group_spec(g, st0, k_ref_, vt_ref_, bias_chunk):
        return (g * J, J, st0 + g * J,
                lambda kc: k_ref_[0, g, rows(kc), :],
                lambda kc: vt_ref_[0, kc, g * VROWS:(g + 1) * VROWS, :],
                bias_chunk)

    sel_groups = [group_spec(g, 0, ks_ref, vst_ref, lambda kc, g=g: bsel_s[g, rows(kc), :])
                  for g in range(NSA_GROUPS)]
    win_groups = [group_spec(g, N_TOK_HEADS, kw_ref, vwt_ref, win_bias) for g in range(NSA_GROUPS)]
    _attend_init(2 * N_TOK_HEADS, m_s, osw_s)
    _attend_chunks(sel_groups, qrot_ref, 0, lo, m_s, osw_s)
    _attend_chunks(sel_groups + win_groups, qrot_ref, lo, nk, m_s, osw_s)

    def gated(h):
        return (gt[3 * h:3 * h + 1, :] * oc_s[h]
                + gt[3 * h + 1:3 * h + 2, :] * _normalised(osw_s[h])
                + gt[3 * h + 2:3 * h + 3, :] * _normalised(osw_s[N_TOK_HEADS + h]))

    for h in range(0, N_TOK_HEADS, 2):
        _write_pair_t(out_ref, h * HEAD_DIM, gated(h), gated(h + 1))
    _mem_attention_t(qm_ref, km_ref, vmt_ref, out_ref)


def _nsa_attn(qraw, qrot, kcmp, vcmp, ks, vst, kw, vwt, gt, qm, km, vmt, B, S):
    G = NSA_GROUPS
    NM = km.shape[2]
    NC = kcmp.shape[2]
    n_cmp = (S - CMP_LEN) // CMP_STRIDE + 1
    n_slc = S // SLC_LEN
    n_sel = min(SLC_TOP_MAX, n_slc)
    nsp = -(-n_slc // 8) * 8
    c0 = np.arange(NC) * CMP_STRIDE
    s0 = np.arange(nsp) * SLC_LEN
    ov = np.minimum(c0[None, :] + CMP_LEN, s0[:, None] + SLC_LEN) - np.maximum(c0[None, :], s0[:, None])
    ovt = (np.clip(ov, 0, None) / CMP_LEN).astype(np.float32)
    ovt[:, n_cmp:] = 0.0
    ovt[n_slc:, :] = 0.0

    qblk = lambda nh: pl.BlockSpec((1, nh, TQ, HEAD_DIM), lambda b, i: (b, 0, i, 0))
    full = pl.BlockSpec((1, G, S, HEAD_DIM), lambda b, i: (b, 0, 0, 0))
    full_t = pl.BlockSpec((1, S // TK, G * VROWS, TK), lambda b, i: (b, 0, 0, 0))
    cmpspec = pl.BlockSpec((1, G, NC, HEAD_DIM), lambda b, i: (b, 0, 0, 0))
    head_out = pltpu.VMEM((N_TOK_HEADS, HEAD_DIM, TQ), F32)
    head_acc = pltpu.VMEM((2 * N_TOK_HEADS, VROWS, TQ), F32)
    stat = pltpu.VMEM((2 * N_TOK_HEADS, 1, TQ), F32)
    return pl.pallas_call(
        functools.partial(_nsa_attn_kernel, n_cmp=n_cmp, n_slc=n_slc, n_sel=n_sel),
        grid=(B, S // TQ),
        in_specs=[
            qblk(N_TOK_HEADS), qblk(N_TOK_HEADS), cmpspec, cmpspec, full, full_t, full, full_t,
            pl.BlockSpec((1, GATE_ROWS, TQ), lambda b, i: (b, 0, i)),
            pl.BlockSpec((nsp, NC), lambda b, i: (0, 0)),
            qblk(N_MEM_HEADS),
            pl.BlockSpec((1, N_MEM_HEADS, NM, HEAD_DIM), lambda b, i: (b, 0, 0, 0)),
            pl.BlockSpec((1, N_MEM_HEADS * VROWS, NM), lambda b, i: (b, 0, 0)),
        ],
        out_specs=pl.BlockSpec((1, TQ, TOK_W + MEM_Q), lambda b, i: (b, i, 0)),
        out_shape=jax.ShapeDtypeStruct((B, S, TOK_W + MEM_Q), BF16),
        scratch_shapes=[
            pltpu.VMEM((G, nsp, TQ), F32),
            pltpu.VMEM((G, S, TQ), BF16),
            pltpu.VMEM(((WIN // TQ + 1 + TK // TQ) * TQ, TQ), BF16),
            stat,
            head_out, head_acc,
        ],
        compiler_params=_cparams(2),
        name="nsa_attn",
    )(qraw, qrot, kcmp, vcmp, ks, vst, kw, vwt, gt, jnp.asarray(ovt), qm, km, vmt)


def _post_attn_kernel(x_ref, mix_ref, wo_ref, g_ref, win_ref, wdn_ref, gf_ref, out_ref, *, final):
    d_ff = wdn_ref.shape[0]
    x1 = x_ref[...] + jnp.dot(mix_ref[...], wo_ref[...], preferred_element_type=F32)
    h = _rms(x1, g_ref[...]).astype(BF16)
    gate = jnp.dot(h, win_ref[:, :d_ff], preferred_element_type=F32)
    up = jnp.dot(h, win_ref[:, d_ff:], preferred_element_type=F32)
    act = (jax.nn.silu(gate) * up).astype(BF16)
    x2 = x1 + jnp.dot(act, wdn_ref[...], preferred_element_type=F32)
    if final:
        x2 = _rms(x2, gf_ref[...])
    out_ref[...] = x2


def _post_attn(x2d, mix2d, w_o, g, w_in, w_down, g_final, final):
    N, D = x2d.shape
    MW = mix2d.shape[1]
    d_ff = w_down.shape[0]
    tm = min(TM_FFN, N)
    const = lambda shp: pl.BlockSpec(shp, lambda i: (0, 0), pipeline_mode=pl.Buffered(1))
    return pl.pallas_call(
        functools.partial(_post_attn_kernel, final=final),
        grid=(N // tm,),
        in_specs=[
            pl.BlockSpec((tm, D), lambda i: (i, 0)),
            pl.BlockSpec((tm, MW), lambda i: (i, 0)),
            const((MW, D)), const((1, D)), const((D, 2 * d_ff)), const((d_ff, D)), const((1, D)),
        ],
        out_specs=pl.BlockSpec((tm, D), lambda i: (i, 0)),
        out_shape=jax.ShapeDtypeStruct((N, D), F32),
        compiler_params=_cparams(1),
        name="post_attn",
    )(x2d, mix2d, w_o.astype(BF16), g.reshape(1, D), w_in.astype(BF16), w_down.astype(BF16),
      g_final.reshape(1, D))


def kernel(x, mem, attn_norm, mem_norm, ffn_norm, final_norm, dsa_w_in, dsa_ckv_norm, dsa_w_uk, dsa_w_uv, nsa_w_in, nsa_cmp_pos_k, nsa_cmp_pos_v, nsa_cmp_k_w1, nsa_cmp_k_w2, nsa_cmp_v_w1, nsa_cmp_v_w2, mem_w_kv, w_o, ffn_w_in, ffn_w_down):
    B, S, D = x.shape
    depth = attn_norm.shape[0]
    assert S % TM_PROJ == 0 or S < TM_PROJ
    assert S % TQ == 0 and (B * S) % TM_FFN == 0
    tabs = _rope_tables(S)
    x2d = x.reshape(B * S, D)
    for i in range(depth):
        km, vm = _mem_kv(mem, mem_norm[i], mem_w_kv[i])
        if i % 2 == 0:
            a = i // 2
            q, qi, kidx, k, v, wt, qm = _proj_dsa(
                x2d, attn_norm[i], dsa_w_in[a], dsa_ckv_norm[a], dsa_w_uk[a], dsa_w_uv[a], tabs, B, S)
            mix = _dsa_attn(q, qi, wt, kidx, k, v, qm, km, vm, B, S)
        else:
            b = i // 2
            qraw, qrot, kc, vc, ks, vs, kw, vw, gt, qm = _proj_nsa(
                x2d, attn_norm[i], nsa_w_in[b], tabs, B, S)
            kcmp, vcmp = _nsa_compress(
                kc, vc, nsa_cmp_pos_k[b], nsa_cmp_pos_v[b],
                nsa_cmp_k_w1[b], nsa_cmp_k_w2[b], nsa_cmp_v_w1[b], nsa_cmp_v_w2[b], B, S)
            mix = _nsa_attn(qraw, qrot, kcmp, vcmp, ks, vs, kw, vw, gt, qm, km, vm, B, S)
        x2d = _post_attn(x2d, mix.reshape(B * S, TOK_W + MEM_Q), w_o[i], ffn_norm[i],
                         ffn_w_in[i], ffn_w_down[i], final_norm, final=(i == depth - 1))
    return x2d.reshape(B, S, D)
```

```python
import functools
import math

import numpy as np
import jax
import jax.numpy as jnp
from jax import lax
from jax.experimental import pallas as pl
from jax.experimental.pallas import tpu as pltpu

F32 = jnp.float32
BF16 = jnp.bfloat16

HEAD_DIM = 64
ROPE_DIM = 16
ROPE_THETA = 500000.0
RMS_EPS = 1e-6
N_TOK_HEADS = 12
N_MEM_HEADS = 4
TOK_W = N_TOK_HEADS * HEAD_DIM
MEM_Q = N_MEM_HEADS * HEAD_DIM
DSA_KV_RANK = 128
DSA_NOPE = HEAD_DIM - ROPE_DIM
IDX_HEADS = 8
IDX_DIM = 64
DSA_TOPK_MAX = 256
NSA_GROUPS = 2
NSA_HPG = N_TOK_HEADS // NSA_GROUPS
CMP_LEN = 32
CMP_STRIDE = 16
CMP_HIDDEN = 128
SLC_LEN = 64
SLC_TOP_MAX = 16
WIN = 512
FORCE_SCORE = 1e9
N_GATES = N_TOK_HEADS * 3
GATE_ROWS = 40

LANE = 128
TQ = 256
TK = 256
ONES_ROWS = 16
VROWS = HEAD_DIM + ONES_ROWS
TM_PROJ = 512
TM_FFN = 512
VMEM_LIMIT = 56 * 1024 * 1024
NEG = -1e30
SCALE_LOG2 = HEAD_DIM ** -0.5 * math.log2(math.e)
INT_MIN = -2 ** 31

_NT = (((1,), (1,)), ((), ()))
_TN = (((0,), (0,)), ((), ()))


def _cparams(n_axes):
    return pltpu.CompilerParams(
        dimension_semantics=("arbitrary",) * n_axes, vmem_limit_bytes=VMEM_LIMIT)


def _rms(xf, g):
    ms = jnp.mean(xf * xf, axis=-1, keepdims=True)
    return xf * lax.rsqrt(ms + RMS_EPS) * g


def _rope_tables(S):
    inv = ROPE_THETA ** (-np.arange(0, ROPE_DIM, 2, dtype=np.float64) / ROPE_DIM)
    ang = np.arange(S, dtype=np.float64)[:, None] * inv[None, :]
    cos = np.cos(ang).astype(np.float32)
    sin = np.sin(ang).astype(np.float32)
    half = ROPE_DIM // 2
    c = np.ones((S, LANE), np.float32)
    s_up = np.zeros((S, LANE), np.float32)
    s_dn = np.zeros((S, LANE), np.float32)
    for base in range(0, LANE, HEAD_DIM):
        c[:, base:base + half] = cos
        c[:, base + half:base + ROPE_DIM] = cos
        s_up[:, base + half:base + ROPE_DIM] = sin
        s_dn[:, base:base + half] = -sin
    return jnp.asarray(c), jnp.asarray(s_up), jnp.asarray(s_dn)


def _rope(x, c, s_up, s_dn):
    half = ROPE_DIM // 2
    outs = []
    for j in range(x.shape[1] // LANE):
        xc = x[:, j * LANE:(j + 1) * LANE]
        outs.append(xc * c + pltpu.roll(xc, half, 1) * s_up + pltpu.roll(xc, LANE - half, 1) * s_dn)
    return outs


def _store_heads(ref, chunks, dtype):
    for j, ch in enumerate(chunks):
        ref[0, 2 * j, :, :] = ch[:, :HEAD_DIM].astype(dtype)
        ref[0, 2 * j + 1, :, :] = ch[:, HEAD_DIM:].astype(dtype)


def _chunks(x):
    return [x[:, j * LANE:(j + 1) * LANE] for j in range(x.shape[1] // LANE)]


def _store_values_t(ref, x, row_lo, n_groups):
    xt = x.T
    ones = jnp.ones((ONES_ROWS, TK), ref.dtype)
    for j in range(x.shape[0] // TK):
        for g in range(n_groups):
            r0 = row_lo + g * HEAD_DIM
            ref[0, j, g * VROWS:g * VROWS + HEAD_DIM, :] = (
                xt[r0:r0 + HEAD_DIM, j * TK:(j + 1) * TK].astype(ref.dtype))
            ref[0, j, g * VROWS + HEAD_DIM:(g + 1) * VROWS, :] = ones


def _proj_dsa_kernel(x_ref, g_ref, w_ref, gkv_ref, wkv_ref, c_ref, su_ref, sd_ref,
                     q_ref, qi_ref, kidx_ref, k_ref, vt_ref, wt_ref, qm_ref):
    h = _rms(x_ref[...], g_ref[...]).astype(BF16)
    c, su, sd = c_ref[...], su_ref[...], sd_ref[...]

    def mm(lo, hi):
        return jnp.dot(h, w_ref[:, lo:hi], preferred_element_type=F32)

    q = _rope(mm(0, TOK_W), c, su, sd)
    _store_heads(q_ref, [ch * SCALE_LOG2 for ch in q], BF16)

    ckv = _rms(mm(TOK_W, TOK_W + DSA_KV_RANK), gkv_ref[...]).astype(BF16)
    kvn = jnp.dot(ckv, wkv_ref[...], preferred_element_type=F32)

    o_qi = TOK_W + DSA_KV_RANK
    qi = _rope(mm(o_qi, o_qi + IDX_HEADS * IDX_DIM), c, su, sd)
    _store_heads(qi_ref, qi, BF16)

    o_misc = o_qi + IDX_HEADS * IDX_DIM
    misc = _rope(mm(o_misc, o_misc + LANE), c, su, sd)[0]
    kidx_ref[0] = misc[:, :IDX_DIM].astype(BF16)
    lane = lax.broadcasted_iota(jnp.int32, misc.shape, 1)
    k_full = jnp.where(lane < ROPE_DIM, pltpu.roll(misc, HEAD_DIM, 1), kvn)
    k_ref[0] = k_full[:, :HEAD_DIM].astype(BF16)
    _store_values_t(vt_ref, kvn, HEAD_DIM, 1)
    w_scaled = misc * (IDX_HEADS ** -0.5 * IDX_DIM ** -0.5)
    w_row0 = IDX_DIM + ROPE_DIM
    wt_ref[0] = w_scaled.T[w_row0:w_row0 + IDX_HEADS, :]

    o_qm = o_misc + LANE
    qm = mm(o_qm, o_qm + MEM_Q)
    _store_heads(qm_ref, [ch * SCALE_LOG2 for ch in _chunks(qm)], BF16)


def _proj_dsa(x2d, g, w_in, gkv, w_uk, w_uv, tabs, B, S):
    D = x2d.shape[1]
    o = np.cumsum([0, TOK_W, DSA_KV_RANK, ROPE_DIM, IDX_HEADS * IDX_DIM, IDX_DIM, IDX_HEADS, MEM_Q])
    q, ckv, kr, qi, ki, wi, qm = [w_in[:, o[i]:o[i + 1]] for i in range(7)]
    pad = jnp.zeros((D, LANE - IDX_DIM - ROPE_DIM - IDX_HEADS), w_in.dtype)
    w = jnp.concatenate([q, ckv, qi, ki, kr, wi, pad, qm], axis=1).astype(BF16)
    wkv = jnp.concatenate([jnp.zeros((DSA_KV_RANK, ROPE_DIM), F32), w_uk, w_uv], axis=1).astype(BF16)
    ncol = w.shape[1]
    tm = min(2 * TM_PROJ, S)
    nst = S // tm
    tok = lambda i: (i, 0)
    const = lambda i: (0, 0)
    tab = lambda i: (i % nst, 0)
    hm = lambda i: (i // nst, 0, i % nst, 0)
    row = lambda i: (i // nst, i % nst, 0)
    outs = pl.pallas_call(
        _proj_dsa_kernel,
        grid=(B * nst,),
        in_specs=[
            pl.BlockSpec((tm, D), tok),
            pl.BlockSpec((1, D), const),
            pl.BlockSpec((D, ncol), const),
            pl.BlockSpec((1, DSA_KV_RANK), const),
            pl.BlockSpec((DSA_KV_RANK, LANE), const),
            pl.BlockSpec((tm, LANE), tab),
            pl.BlockSpec((tm, LANE), tab),
            pl.BlockSpec((tm, LANE), tab),
        ],
        out_specs=[
            pl.BlockSpec((1, N_TOK_HEADS, tm, HEAD_DIM), hm),
            pl.BlockSpec((1, IDX_HEADS, tm, IDX_DIM), hm),
            pl.BlockSpec((1, tm, IDX_DIM), row),
            pl.BlockSpec((1, tm, HEAD_DIM), row),
            pl.BlockSpec((1, tm // TK, VROWS, TK), lambda i: (i // nst, i % nst, 0, 0)),
            pl.BlockSpec((1, IDX_HEADS, tm), lambda i: (i // nst, 0, i % nst)),
            pl.BlockSpec((1, N_MEM_HEADS, tm, HEAD_DIM), hm),
        ],
        out_shape=[
            jax.ShapeDtypeStruct((B, N_TOK_HEADS, S, HEAD_DIM), BF16),
            jax.ShapeDtypeStruct((B, IDX_HEADS, S, IDX_DIM), BF16),
            jax.ShapeDtypeStruct((B, S, IDX_DIM), BF16),
            jax.ShapeDtypeStruct((B, S, HEAD_DIM), BF16),
            jax.ShapeDtypeStruct((B, S // TK, VROWS, TK), BF16),
            jax.ShapeDtypeStruct((B, IDX_HEADS, S), F32),
            jax.ShapeDtypeStruct((B, N_MEM_HEADS, S, HEAD_DIM), BF16),
        ],
        compiler_params=_cparams(1),
        name="proj_dsa",
    )(x2d, g.reshape(1, D), w, gkv.reshape(1, DSA_KV_RANK), wkv, *tabs)
    return outs


def _proj_nsa_kernel(x_ref, g_ref, w_ref, c_ref, su_ref, sd_ref,
                     qraw_ref, qrot_ref, kc_ref, vc_ref, ks_ref, vs_ref, kw_ref, vw_ref,
                     gt_ref, qm_ref):
    h = _rms(x_ref[...], g_ref[...]).astype(BF16)
    c, su, sd = c_ref[...], su_ref[...], sd_ref[...]

    def mm(lo, hi):
        return jnp.dot(h, w_ref[:, lo:hi], preferred_element_type=F32)

    q = mm(0, TOK_W)
    _store_heads(qraw_ref, [ch * SCALE_LOG2 for ch in _chunks(q)], BF16)
    _store_heads(qrot_ref, [ch * SCALE_LOG2 for ch in _rope(q, c, su, sd)], BF16)

    o = TOK_W
    _store_heads(kc_ref, _chunks(mm(o, o + LANE)), F32)
    _store_heads(vc_ref, _chunks(mm(o + LANE, o + 2 * LANE)), F32)
    _store_heads(ks_ref, _rope(mm(o + 2 * LANE, o + 3 * LANE), c, su, sd), BF16)
    _store_values_t(vs_ref, mm(o + 3 * LANE, o + 4 * LANE), 0, NSA_GROUPS)
    _store_heads(kw_ref, _rope(mm(o + 4 * LANE, o + 5 * LANE), c, su, sd), BF16)
    _store_values_t(vw_ref, mm(o + 5 * LANE, o + 6 * LANE), 0, NSA_GROUPS)

    gates = jax.nn.sigmoid(mm(o + 6 * LANE, o + 7 * LANE))
    gt_ref[0] = gates.T[:GATE_ROWS, :]

    qm = mm(o + 7 * LANE, o + 7 * LANE + MEM_Q)
    _store_heads(qm_ref, [ch * SCALE_LOG2 for ch in _chunks(qm)], BF16)


def _proj_nsa(x2d, g, w_in, tabs, B, S):
    D = x2d.shape[1]
    kv_w = NSA_GROUPS * HEAD_DIM
    o_g = TOK_W + 6 * kv_w
    pad = jnp.zeros((D, LANE - N_GATES), w_in.dtype)
    w = jnp.concatenate([w_in[:, :o_g + N_GATES], pad, w_in[:, o_g + N_GATES:]], axis=1).astype(BF16)
    ncol = w.shape[1]
    tm = min(TM_PROJ, S)
    nst = S // tm
    tok = lambda i: (i, 0)
    const = lambda i: (0, 0)
    tab = lambda i: (i % nst, 0)
    hm = lambda i: (i // nst, 0, i % nst, 0)

    def hm_spec(nh):
        return pl.BlockSpec((1, nh, tm, HEAD_DIM), hm)

    def hm_shape(nh, dt):
        return jax.ShapeDtypeStruct((B, nh, S, HEAD_DIM), dt)

    G = NSA_GROUPS
    vt_spec = pl.BlockSpec((1, tm // TK, G * VROWS, TK), lambda i: (i // nst, i % nst, 0, 0))
    vt_shape = jax.ShapeDtypeStruct((B, S // TK, G * VROWS, TK), BF16)
    outs = pl.pallas_call(
        _proj_nsa_kernel,
        grid=(B * nst,),
        in_specs=[
            pl.BlockSpec((tm, D), tok),
            pl.BlockSpec((1, D), const),
            pl.BlockSpec((D, ncol), const),
            pl.BlockSpec((tm, LANE), tab),
            pl.BlockSpec((tm, LANE), tab),
            pl.BlockSpec((tm, LANE), tab),
        ],
        out_specs=[
            hm_spec(N_TOK_HEADS), hm_spec(N_TOK_HEADS),
            hm_spec(G), hm_spec(G), hm_spec(G), vt_spec, hm_spec(G), vt_spec,
            pl.BlockSpec((1, GATE_ROWS, tm), lambda i: (i // nst, 0, i % nst)),
            hm_spec(N_MEM_HEADS),
        ],
        out_shape=[
            hm_shape(N_TOK_HEADS, BF16), hm_shape(N_TOK_HEADS, BF16),
            hm_shape(G, F32), hm_shape(G, F32),
            hm_shape(G, BF16), vt_shape, hm_shape(G, BF16), vt_shape,
            jax.ShapeDtypeStruct((B, GATE_ROWS, S), F32),
            hm_shape(N_MEM_HEADS, BF16),
        ],
        compiler_params=_cparams(1),
        name="proj_nsa",
    )(x2d, g.reshape(1, D), w, *tabs)
    return outs


def _mem_kv_kernel(m_ref, g_ref, w_ref, k_ref, vt_ref):
    h = _rms(m_ref[0], g_ref[...]).astype(BF16)
    kv = jnp.dot(h, w_ref[...], preferred_element_type=F32)
    _store_heads(k_ref, _chunks(kv[:, :MEM_Q]), BF16)
    ones = jnp.ones((ONES_ROWS, kv.shape[0]), BF16)
    for j in range(MEM_Q // LANE):
        pair_t = kv[:, MEM_Q + j * LANE:MEM_Q + (j + 1) * LANE].T.astype(BF16)
        for r in range(LANE // HEAD_DIM):
            h = j * (LANE // HEAD_DIM) + r
            vt_ref[0, h * VROWS:h * VROWS + HEAD_DIM, :] = pair_t[r * HEAD_DIM:(r + 1) * HEAD_DIM, :]
            vt_ref[0, h * VROWS + HEAD_DIM:(h + 1) * VROWS, :] = ones


def _mem_kv(mem, g, w_kv):
    B, NM, D = mem.shape
    assert NM % LANE == 0
    spec = pl.BlockSpec((1, N_MEM_HEADS, NM, HEAD_DIM), lambda b: (b, 0, 0, 0))
    shape = jax.ShapeDtypeStruct((B, N_MEM_HEADS, NM, HEAD_DIM), BF16)
    vt_spec = pl.BlockSpec((1, N_MEM_HEADS * VROWS, NM), lambda b: (b, 0, 0))
    vt_shape = jax.ShapeDtypeStruct((B, N_MEM_HEADS * VROWS, NM), BF16)
    return pl.pallas_call(
        _mem_kv_kernel,
        grid=(B,),
        in_specs=[
            pl.BlockSpec((1, NM, D), lambda b: (b, 0, 0)),
            pl.BlockSpec((1, D), lambda b: (0, 0)),
            pl.BlockSpec((D, 2 * MEM_Q), lambda b: (0, 0)),
        ],
        out_specs=[spec, vt_spec],
        out_shape=[shape, vt_shape],
        compiler_params=_cparams(1),
        name="mem_kv",
    )(mem, g.reshape(1, D), w_kv.astype(BF16))


def _write_pair_t(out_ref, col, o_a, o_b):
    pair = jnp.concatenate([o_a, o_b], axis=0)
    out_ref[0, :, col:col + 2 * HEAD_DIM] = pair.T.astype(out_ref.dtype)


def _mem_attention_t(qm_ref, km_ref, vmt_ref, out_ref):
    scores = [lax.dot_general(km_ref[0, h], qm_ref[0, h], _NT, preferred_element_type=F32)
              for h in range(N_MEM_HEADS)]
    outs = []
    for h in range(N_MEM_HEADS):
        s = scores[h].astype(BF16)
        p = jnp.exp2(s - jnp.max(s, axis=0, keepdims=True))
        vt = vmt_ref[0, h * VROWS:(h + 1) * VROWS, :]
        outs.append(_normalised(jnp.dot(vt, p, preferred_element_type=F32)))
    for h in range(0, N_MEM_HEADS, 2):
        _write_pair_t(out_ref, TOK_W + h * HEAD_DIM, outs[h], outs[h + 1])


def _normalised(acc):
    return acc[:HEAD_DIM] * (1.0 / acc[HEAD_DIM:HEAD_DIM + 1])


def _attend_init(n_states, m_s, acc_s):
    neg = jnp.full((1, TQ), NEG, BF16).astype(F32)
    for st in range(n_states):
        m_s[st] = neg
        acc_s[st] = jnp.zeros((VROWS, TQ), F32)


def _attend_chunks(groups, q_ref, lo, hi, m_s, acc_s):
    def body(kc, _):
        scores = []
        for (h0, nh, _, k_chunk, _, _) in groups:
            q_g = q_ref[0, h0:h0 + nh].reshape(nh * TQ, HEAD_DIM)
            scores.append(lax.dot_general(k_chunk(kc), q_g, _NT, preferred_element_type=F32))
        for (h0, nh, st0, _, vt_chunk, bias_chunk), s_g in zip(groups, scores):
            for j in range(nh):
                st = st0 + j
                s = s_g[:, j * TQ:(j + 1) * TQ].astype(BF16) + bias_chunk(kc)
                m = m_s[st]
                m_new = jnp.maximum(m, jnp.max(s, axis=0, keepdims=True).astype(F32))
                m_s[st] = m_new
                p = jnp.exp2(s - m_new.astype(BF16))
                pv = jnp.dot(vt_chunk(kc), p, preferred_element_type=F32)
                acc_s[st] = jnp.exp2(m - m_new) * acc_s[st] + pv
        return 0

    lax.fori_loop(lo, hi, body, 0)


def _dsa_attn_kernel(q_ref, qi_ref, wt_ref, kidx_ref, k_ref, vt_ref, qm_ref, km_ref, vmt_ref,
                     out_ref, key_s, top_s, bias_s, m_s, o_s, *, topk):
    i = pl.program_id(1)
    nk = ((i + 1) * TQ + TK - 1) // TK
    t_pos = i * TQ + lax.broadcasted_iota(jnp.int32, (TK, TQ), 1)
    row = lax.broadcasted_iota(jnp.int32, (TK, TQ), 0)
    wt = wt_ref[0]

    def rows(kc):
        return pl.ds(pl.multiple_of(kc * TK, TK), TK)

    def score_body(kc, _):
        kx = kidx_ref[0, rows(kc), :]
        sc = jnp.zeros((TK, TQ), F32)
        for h in range(IDX_HEADS):
            lg = lax.dot_general(kx, qi_ref[0, h], _NT, preferred_element_type=F32)
            sc = sc + jnp.maximum(lg, 0.0) * wt[h:h + 1, :]
        sc = jnp.where(sc == 0.0, 0.0, sc)
        bits = pltpu.bitcast(sc, jnp.int32)
        key = bits ^ ((bits >> 31) & 0x7FFFFFFF)
        valid = kc * TK + row <= t_pos
        key_s[rows(kc), :] = jnp.where(valid, key, INT_MIN)
        top = pltpu.bitcast(bits & -65536, F32)
        top_s[rows(kc), :] = jnp.where(valid, top, -jnp.inf).astype(BF16)
        return 0

    lax.fori_loop(0, nk, score_body, 0)

    n_acc = 4 * 8

    def count(pred_fn):
        def body(kc, c):
            hit = jnp.where(pred_fn(key_s[rows(kc), :]), 1.0, 0.0)
            return c + jnp.sum(hit.reshape(TK // n_acc, n_acc, TQ), axis=0)

        c = lax.fori_loop(0, nk, body, jnp.zeros((n_acc, TQ), F32))
        return jnp.sum(c, axis=0, keepdims=True)

    def count_top(cand):
        one, zero = jnp.ones((), BF16), jnp.zeros((), BF16)

        def body(kc, c):
            hit = jnp.where(top_s[rows(kc), :] >= cand, one, zero)
            parts = [hit[r * n_acc:(r + 1) * n_acc] for r in range(TK // n_acc)]
            while len(parts) > 1:
                parts = [a + b for a, b in zip(parts[0::2], parts[1::2])]
            return c + parts[0].astype(F32)

        c = lax.fori_loop(0, nk, body, jnp.zeros((n_acc, TQ), F32))
        return jnp.sum(c, axis=0, keepdims=True)

    def top_bit_body(b, carry):
        ans, n_ge = carry
        cand = ans + jnp.left_shift(jnp.int32(1), 15 - b)
        cand_bits = jnp.left_shift(cand ^ ((cand >> 31) & 0x7FFF), 16)
        cnt = count_top(pltpu.bitcast(cand_bits, F32).astype(BF16))
        take = cnt >= topk
        return jnp.where(take, cand, ans), jnp.where(take, cnt, n_ge)

    def bit_body(b, carry):
        ans, n_ge = carry
        cand = ans + jnp.left_shift(jnp.int32(1), 31 - b)
        cnt = count(lambda keys: keys >= cand)
        take = cnt >= topk
        return jnp.where(take, cand, ans), jnp.where(take, cnt, n_ge)

    top, n_ge = lax.fori_loop(
        0, 16, top_bit_body,
        (jnp.full((1, TQ), -2 ** 15, jnp.int32), jnp.full((1, TQ), float(topk), F32)))
    thr, n_ge = lax.fori_loop(16, 32, bit_body, (jnp.left_shift(top, 16), n_ge))
    has_ties = jnp.max(jnp.where((n_ge > topk) & (thr > INT_MIN), 1.0, 0.0)) > 0.0

    @pl.when(jnp.logical_not(has_ties))
    def _():
        thr_valid = jnp.maximum(thr, INT_MIN + 1)

        def bias_body(kc, _):
            bias_s[rows(kc), :] = jnp.where(key_s[rows(kc), :] >= thr_valid, 0.0, NEG).astype(BF16)
            return 0

        lax.fori_loop(0, nk, bias_body, 0)

    @pl.when(has_ties)
    def _():
        need = topk - count(lambda keys: keys > thr)
        tri = (lax.broadcasted_iota(jnp.int32, (TK, TK), 1)
               < lax.broadcasted_iota(jnp.int32, (TK, TK), 0)).astype(BF16)

        def bias_body(kc, eq_seen):
            keys = key_s[rows(kc), :]
            eq = keys == thr
            eq_f = jnp.where(eq, 1.0, 0.0)
            before = eq_seen + jnp.dot(tri, eq_f.astype(BF16), preferred_element_type=F32)
            sel = (keys > thr) | (eq & (before < need))
            sel = sel & (kc * TK + row <= t_pos)
            bias_s[rows(kc), :] = jnp.where(sel, 0.0, NEG).astype(BF16)
            return eq_seen + jnp.sum(eq_f, axis=0, keepdims=True)

        lax.fori_loop(0, nk, bias_body, jnp.zeros((1, TQ), F32))

    _attend_init(N_TOK_HEADS, m_s, o_s)
    _attend_chunks(
        [(0, N_TOK_HEADS, 0,
          lambda kc: k_ref[0, rows(kc), :],
          lambda kc: vt_ref[0, kc],
          lambda kc: bias_s[rows(kc), :])],
        q_ref, 0, nk, m_s, o_s)
    for h in range(0, N_TOK_HEADS, 2):
        _write_pair_t(out_ref, h * HEAD_DIM, _normalised(o_s[h]), _normalised(o_s[h + 1]))
    _mem_attention_t(qm_ref, km_ref, vmt_ref, out_ref)


def _dsa_attn(q, qi, wt, kidx, k, vt, qm, km, vmt, B, S):
    NM = km.shape[2]
    topk = min(DSA_TOPK_MAX, S // 4)
    assert S % TK == 0
    key_rows = S
    stat = pltpu.VMEM((N_TOK_HEADS, 1, TQ), F32)
    qblk = lambda nh: pl.BlockSpec((1, nh, TQ, HEAD_DIM), lambda b, i: (b, 0, i, 0))
    full = pl.BlockSpec((1, S, HEAD_DIM), lambda b, i: (b, 0, 0))
    return pl.pallas_call(
        functools.partial(_dsa_attn_kernel, topk=topk),
        grid=(B, S // TQ),
        in_specs=[
            qblk(N_TOK_HEADS), qblk(IDX_HEADS),
            pl.BlockSpec((1, IDX_HEADS, TQ), lambda b, i: (b, 0, i)),
            full, full,
            pl.BlockSpec((1, S // TK, VROWS, TK), lambda b, i: (b, 0, 0, 0)),
            qblk(N_MEM_HEADS),
            pl.BlockSpec((1, N_MEM_HEADS, NM, HEAD_DIM), lambda b, i: (b, 0, 0, 0)),
            pl.BlockSpec((1, N_MEM_HEADS * VROWS, NM), lambda b, i: (b, 0, 0)),
        ],
        out_specs=pl.BlockSpec((1, TQ, TOK_W + MEM_Q), lambda b, i: (b, i, 0)),
        out_shape=jax.ShapeDtypeStruct((B, S, TOK_W + MEM_Q), BF16),
        scratch_shapes=[
            pltpu.VMEM((key_rows, TQ), jnp.int32),
            pltpu.VMEM((key_rows, TQ), BF16),
            pltpu.VMEM((S, TQ), BF16),
            stat,
            pltpu.VMEM((N_TOK_HEADS, VROWS, TQ), F32),
        ],
        compiler_params=_cparams(2),
        name="dsa_attn",
    )(q, qi, wt, kidx, k, vt, qm, km, vmt)


def _nsa_compress_kernel(kc_ref, vc_ref, pk_ref, pv_ref, kw1_ref, kw2_ref, vw1_ref, vw2_ref,
                         ko_ref, vo_ref):
    half = (CMP_LEN // 2) * HEAD_DIM

    def run(x_ref, pos_ref, w1_ref, w2_ref, o_ref):
        n_rows = x_ref.shape[2] // CMP_STRIDE
        r = jnp.concatenate(
            [x_ref[0, 0, pl.ds(l, n_rows, stride=CMP_STRIDE), :] for l in range(CMP_STRIDE)], axis=1)
        nxt = pltpu.roll(r, r.shape[0] - 1, 0)
        a = (r + pos_ref[0:1, :]).astype(BF16)
        b = (nxt + pos_ref[1:2, :]).astype(BF16)
        hid = (jnp.dot(a, w1_ref[:half, :], preferred_element_type=F32)
               + jnp.dot(b, w1_ref[half:, :], preferred_element_type=F32))
        hid = jax.nn.gelu(hid).astype(BF16)
        o_ref[0, 0] = jnp.dot(hid, w2_ref[...], preferred_element_type=F32).astype(o_ref.dtype)

    run(kc_ref, pk_ref, kw1_ref, kw2_ref, ko_ref)
    run(vc_ref, pv_ref, vw1_ref, vw2_ref, vo_ref)


def _nsa_compress(kc, vc, pos_k, pos_v, k_w1, k_w2, v_w1, v_w2, B, S):
    G = NSA_GROUPS
    R = S // CMP_STRIDE
    W = CMP_STRIDE * HEAD_DIM
    xs = pl.BlockSpec((1, 1, S, HEAD_DIM), lambda b, g: (b, g, 0, 0))
    cst = lambda shp: pl.BlockSpec(shp, lambda b, g: (0, 0))
    osz = pl.BlockSpec((1, 1, R, HEAD_DIM), lambda b, g: (b, g, 0, 0))
    osh = jax.ShapeDtypeStruct((B, G, R, HEAD_DIM), BF16)
    return pl.pallas_call(
        _nsa_compress_kernel,
        grid=(B, G),
        in_specs=[xs, xs, cst((2, W)), cst((2, W)),
                  cst((CMP_LEN * HEAD_DIM, CMP_HIDDEN)), cst((CMP_HIDDEN, HEAD_DIM)),
                  cst((CMP_LEN * HEAD_DIM, CMP_HIDDEN)), cst((CMP_HIDDEN, HEAD_DIM))],
        out_specs=[osz, osz],
        out_shape=[osh, osh],
        compiler_params=_cparams(2),
        name="nsa_compress",
    )(kc, vc, pos_k.reshape(2, W), pos_v.reshape(2, W),
      k_w1.astype(BF16), k_w2.astype(BF16), v_w1.astype(BF16), v_w2.astype(BF16))


def _nsa_attn_kernel(qraw_ref, qrot_ref, kcmp_ref, vcmp_ref, ks_ref, vst_ref, kw_ref, vwt_ref,
                     gt_ref, ovt_ref, qm_ref, km_ref, vmt_ref, out_ref,
                     sel_s, bsel_s, wb_s, m_s, oc_s, osw_s, *, n_cmp, n_slc, n_sel):
    i = pl.program_id(1)
    J = NSA_HPG
    L = J * TQ
    NC = kcmp_ref.shape[2]
    NSP = sel_s.shape[1]
    t_row = i * TQ + lax.broadcasted_iota(jnp.int32, (1, TQ), 1)
    t_pos = i * TQ + lax.broadcasted_iota(jnp.int32, (TK, TQ), 1)
    row = lax.broadcasted_iota(jnp.int32, (TK, TQ), 0)
    gt = gt_ref[0]

    c_idx = lax.broadcasted_iota(jnp.int32, (NC, TQ), 0)
    t_c = i * TQ + lax.broadcasted_iota(jnp.int32, (NC, TQ), 1)
    mask_c1 = (c_idx * CMP_STRIDE + CMP_LEN - 1 <= t_c) & (c_idx < n_cmp)
    mask_c = jnp.concatenate([mask_c1] * J, axis=1)

    n_idx = lax.broadcasted_iota(jnp.int32, (NSP, TQ), 0)
    cur = t_row // SLC_LEN
    forced = (n_idx == 0) | (n_idx == cur) | (n_idx == cur - 1)
    admissible = (n_idx <= cur) & (n_idx < n_slc)

    for g in range(NSA_GROUPS):
        h0 = g * J
        q_raw = qraw_ref[0, h0:h0 + J].reshape(L, HEAD_DIM)
        s_c = lax.dot_general(kcmp_ref[0, g], q_raw, _NT, preferred_element_type=F32)
        s_c = jnp.where(mask_c, s_c, -jnp.inf)
        m_c = jnp.max(s_c, axis=0, keepdims=True)
        m_c = jnp.where(m_c > -jnp.inf, m_c, 0.0)
        p_c = jnp.exp2(s_c - m_c)
        p_c = p_c * (1.0 / jnp.maximum(jnp.sum(p_c, axis=0, keepdims=True), 1e-30))
        o_c = lax.dot_general(vcmp_ref[0, g], p_c.astype(BF16), _TN, preferred_element_type=F32)

        p_sum = p_c[:, 0:TQ]
        for j in range(1, J):
            p_sum = p_sum + p_c[:, j * TQ:(j + 1) * TQ]
        imp = jnp.dot(ovt_ref[...], p_sum, preferred_element_type=F32,
                      precision=lax.Precision.HIGHEST)
        imp = jnp.where(forced, FORCE_SCORE, imp)
        imp = jnp.where(admissible, imp, -jnp.inf)
        rank = jnp.zeros((NSP, TQ), F32)
        for mrow in range(n_slc):
            other = imp[mrow:mrow + 1, :]
            ahead = (other > imp) | ((other == imp) & (n_idx > mrow))
            rank = rank + jnp.where(ahead, 1.0, 0.0)
        sel_s[g] = jnp.where(rank < n_sel, 1.0, 0.0)
        for j in range(J):
            oc_s[h0 + j] = o_c[:, j * TQ:(j + 1) * TQ]

    def rows(kc):
        return pl.ds(pl.multiple_of(kc * TK, TK), TK)

    tiles_per_chunk = TK // TQ
    far = WIN // TQ + 1
    rq = lax.broadcasted_iota(jnp.int32, (TQ, TQ), 0)
    cq = lax.broadcasted_iota(jnp.int32, (TQ, TQ), 1)
    for t in range(far + tiles_per_chunk):
        d = far - t
        ok = (rq <= cq + d * TQ) & (rq > cq - WIN + d * TQ)
        wb_s[t * TQ:(t + 1) * TQ, :] = jnp.where(ok, 0.0, NEG).astype(BF16)

    def win_bias(kc):
        t0 = far - i + kc * tiles_per_chunk
        return wb_s[pl.ds(pl.multiple_of(t0 * TQ, TQ), TK), :]

    nk = ((i + 1) * TQ + TK - 1) // TK
    lo = (jnp.maximum(i - WIN // TQ, 0) * TQ) // TK
    per = TK // SLC_LEN
    for g in range(NSA_GROUPS):
        def picked_bias(kc, g=g):
            picked = jnp.concatenate(
                [jnp.broadcast_to(sel_s[g, pl.ds(kc * per + r, 1), :], (SLC_LEN, TQ))
                 for r in range(per)], axis=0)
            return jnp.where(picked > 0.5, 0.0, NEG)

        def full_body(kc, _, g=g, picked_bias=picked_bias):
            bsel_s[g, rows(kc), :] = picked_bias(kc).astype(BF16)
            return 0

        def edge_body(kc, _, g=g, picked_bias=picked_bias):
            bias = jnp.where(kc * TK + row <= t_pos, picked_bias(kc), NEG)
            bsel_s[g, rows(kc), :] = bias.astype(BF16)
            return 0

        n_full = (i * TQ + 1) // TK
        lax.fori_loop(0, n_full, full_body, 0)
        lax.fori_loop(n_full, nk, edge_body, 0)

    def group_spec(g, st0, k_ref_, vt_ref_, bias_chunk):
        return (g * J, J, st0 + g * J,
                lambda kc: k_ref_[0, g, rows(kc), :],
                lambda kc: vt_ref_[0, kc, g * VROWS:(g + 1) * VROWS, :],
                bias_chunk)

    sel_groups = [group_spec(g, 0, ks_ref, vst_ref, lambda kc, g=g: bsel_s[g, rows(kc), :])
                  for g in range(NSA_GROUPS)]
    win_groups = [group_spec(g, N_TOK_HEADS, kw_ref, vwt_ref, win_bias) for g in range(NSA_GROUPS)]
    _attend_init(2 * N_TOK_HEADS, m_s, osw_s)
    _attend_chunks(sel_groups, qrot_ref, 0, lo, m_s, osw_s)
    _attend_chunks(sel_groups + win_groups, qrot_ref, lo, nk, m_s, osw_s)

    def gated(h):
        return (gt[3 * h:3 * h + 1, :] * oc_s[h]
                + gt[3 * h + 1:3 * h + 2, :] * _normalised(osw_s[h])
                + gt[3 * h + 2:3 * h + 3, :] * _normalised(osw_s[N_TOK_HEADS + h]))

    for h in range(0, N_TOK_HEADS, 2):
        _write_pair_t(out_ref, h * HEAD_DIM, gated(h), gated(h + 1))
    _mem_attention_t(qm_ref, km_ref, vmt_ref, out_ref)


def _nsa_attn(qraw, qrot, kcmp, vcmp, ks, vst, kw, vwt, gt, qm, km, vmt, B, S):
    G = NSA_GROUPS
    NM = km.shape[2]
    NC = kcmp.shape[2]
    n_cmp = (S - CMP_LEN) // CMP_STRIDE + 1
    n_slc = S // SLC_LEN
    n_sel = min(SLC_TOP_MAX, n_slc)
    nsp = -(-n_slc // 8) * 8
    c0 = np.arange(NC) * CMP_STRIDE
    s0 = np.arange(nsp) * SLC_LEN
    ov = np.minimum(c0[None, :] + CMP_LEN, s0[:, None] + SLC_LEN) - np.maximum(c0[None, :], s0[:, None])
    ovt = (np.clip(ov, 0, None) / CMP_LEN).astype(np.float32)
    ovt[:, n_cmp:] = 0.0
    ovt[n_slc:, :] = 0.0

    qblk = lambda nh: pl.BlockSpec((1, nh, TQ, HEAD_DIM), lambda b, i: (b, 0, i, 0))
    full = pl.BlockSpec((1, G, S, HEAD_DIM), lambda b, i: (b, 0, 0, 0))
    full_t = pl.BlockSpec((1, S // TK, G * VROWS, TK), lambda b, i: (b, 0, 0, 0))
    cmpspec = pl.BlockSpec((1, G, NC, HEAD_DIM), lambda b, i: (b, 0, 0, 0))
    head_out = pltpu.VMEM((N_TOK_HEADS, HEAD_DIM, TQ), F32)
    head_acc = pltpu.VMEM((2 * N_TOK_HEADS, VROWS, TQ), F32)
    stat = pltpu.VMEM((2 * N_TOK_HEADS, 1, TQ), F32)
    return pl.pallas_call(
        functools.partial(_nsa_attn_kernel, n_cmp=n_cmp, n_slc=n_slc, n_sel=n_sel),
        grid=(B, S // TQ),
        in_specs=[
            qblk(N_TOK_HEADS), qblk(N_TOK_HEADS), cmpspec, cmpspec, full, full_t, full, full_t,
            pl.BlockSpec((1, GATE_ROWS, TQ), lambda b, i: (b, 0, i)),
            pl.BlockSpec((nsp, NC), lambda b, i: (0, 0)),
            qblk(N_MEM_HEADS),
            pl.BlockSpec((1, N_MEM_HEADS, NM, HEAD_DIM), lambda b, i: (b, 0, 0, 0)),
            pl.BlockSpec((1, N_MEM_HEADS * VROWS, NM), lambda b, i: (b, 0, 0)),
        ],
        out_specs=pl.BlockSpec((1, TQ, TOK_W + MEM_Q), lambda b, i: (b, i, 0)),
        out_shape=jax.ShapeDtypeStruct((B, S, TOK_W + MEM_Q), BF16),
        scratch_shapes=[
            pltpu.VMEM((G, nsp, TQ), F32),
            pltpu.VMEM((G, S, TQ), BF16),
            pltpu.VMEM(((WIN // TQ + 1 + TK // TQ) * TQ, TQ), BF16),
            stat,
            head_out, head_acc,
        ],
        compiler_params=_cparams(2),
        name="nsa_attn",
    )(qraw, qrot, kcmp, vcmp, ks, vst, kw, vwt, gt, jnp.asarray(ovt), qm, km, vmt)


def _post_attn_kernel(x_ref, mix_ref, wo_ref, g_ref, win_ref, wdn_ref, gf_ref, out_ref, *, final):
    d_ff = wdn_ref.shape[0]
    x1 = x_ref[...] + jnp.dot(mix_ref[...], wo_ref[...], preferred_element_type=F32)
    h = _rms(x1, g_ref[...]).astype(BF16)
    gate = jnp.dot(h, win_ref[:, :d_ff], preferred_element_type=F32)
    up = jnp.dot(h, win_ref[:, d_ff:], preferred_element_type=F32)
    act = (jax.nn.silu(gate) * up).astype(BF16)
    x2 = x1 + jnp.dot(act, wdn_ref[...], preferred_element_type=F32)
    if final:
        x2 = _rms(x2, gf_ref[...])
    out_ref[...] = x2


def _post_attn(x2d, mix2d, w_o, g, w_in, w_down, g_final, final):
    N, D = x2d.shape
    MW = mix2d.shape[1]
    d_ff = w_down.shape[0]
    tm = min(TM_FFN, N)
    const = lambda shp: pl.BlockSpec(shp, lambda i: (0, 0), pipeline_mode=pl.Buffered(1))
    return pl.pallas_call(
        functools.partial(_post_attn_kernel, final=final),
        grid=(N // tm,),
        in_specs=[
            pl.BlockSpec((tm, D), lambda i: (i, 0)),
            pl.BlockSpec((tm, MW), lambda i: (i, 0)),
            const((MW, D)), const((1, D)), const((D, 2 * d_ff)), const((d_ff, D)), const((1, D)),
        ],
        out_specs=pl.BlockSpec((tm, D), lambda i: (i, 0)),
        out_shape=jax.ShapeDtypeStruct((N, D), F32),
        compiler_params=_cparams(1),
        name="post_attn",
    )(x2d, mix2d, w_o.astype(BF16), g.reshape(1, D), w_in.astype(BF16), w_down.astype(BF16),
      g_final.reshape(1, D))


def kernel(x, mem, attn_norm, mem_norm, ffn_norm, final_norm, dsa_w_in, dsa_ckv_norm, dsa_w_uk, dsa_w_uv, nsa_w_in, nsa_cmp_pos_k, nsa_cmp_pos_v, nsa_cmp_k_w1, nsa_cmp_k_w2, nsa_cmp_v_w1, nsa_cmp_v_w2, mem_w_kv, w_o, ffn_w_in, ffn_w_down):
    B, S, D = x.shape
    depth = attn_norm.shape[0]
    assert S % TM_PROJ == 0 or S < TM_PROJ
    assert S % TQ == 0 and (B * S) % TM_FFN == 0
    tabs = _rope_tables(S)
    x2d = x.reshape(B * S, D)
    for i in range(depth):
        km, vm = _mem_kv(mem, mem_norm[i], mem_w_kv[i])
        if i % 2 == 0:
            a = i // 2
            q, qi, kidx, k, v, wt, qm = _proj_dsa(
                x2d, attn_norm[i], dsa_w_in[a], dsa_ckv_norm[a], dsa_w_uk[a], dsa_w_uv[a], tabs, B, S)
            mix = _dsa_attn(q, qi, wt, kidx, k, v, qm, km, vm, B, S)
        else:
            b = i // 2
            qraw, qrot, kc, vc, ks, vs, kw, vw, gt, qm = _proj_nsa(
                x2d, attn_norm[i], nsa_w_in[b], tabs, B, S)
            kcmp, vcmp = _nsa_compress(
                kc, vc, nsa_cmp_pos_k[b], nsa_cmp_pos_v[b],
                nsa_cmp_k_w1[b], nsa_cmp_k_w2[b], nsa_cmp_v_w1[b], nsa_cmp_v_w2[b], B, S)
            mix = _nsa_attn(qraw, qrot, kcmp, vcmp, ks, vs, kw, vw, gt, qm, km, vm, B, S)
        x2d = _post_attn(x2d, mix.reshape(B * S, TOK_W + MEM_Q), w_o[i], ffn_norm[i],
                         ffn_w_in[i], ffn_w_down[i], final_norm, final=(i == depth - 1))
    return x2d.reshape(B, S, D)
```
